```python
import jax, jax.numpy as jnp
from jax import lax
import numpy as np

D_MODEL = 2048
BATCH = 4
SEQ = 4096
DEPTH = 1

CHUNK = 64
MIX_WIDTH = D_MODEL
GMLP_WIN = 128
A_GROUP_DIM = 128
A_WIDTH = MIX_WIDTH // 2
A_GROUPS = A_WIDTH // A_GROUP_DIM
B_HEAD_DIM = 128
B_WIDTH = MIX_WIDTH - A_WIDTH
B_HEADS = B_WIDTH // B_HEAD_DIM
IDX_HEADS = 16
IDX_HEAD_DIM = 64
TOPK_MAX = 256
Q_BLOCK = 128
ROPE_THETA = 500000.0
B_ROT_DIM = B_HEAD_DIM // 4
IDX_ROT_DIM = IDX_HEAD_DIM // 4
D_FF = 5632
EPS = 1e-6
NEG = -1e30

COL_SIZES = (A_WIDTH, A_WIDTH,
             B_WIDTH, B_WIDTH, B_WIDTH,
             IDX_HEADS * IDX_HEAD_DIM,
             IDX_HEAD_DIM,
             IDX_HEADS)
N_IN_COLS = sum(COL_SIZES)
COL_SPLITS = tuple(int(v) for v in np.cumsum(COL_SIZES)[:-1])

kernel_name = "hybrid_gmlp_dsa_macaron_block"


def _rms(x, gain):
    xf = x.astype(jnp.float32)
    y = xf * lax.rsqrt(jnp.mean(xf * xf, axis=-1, keepdims=True) + EPS)
    return (y * gain.astype(jnp.float32)).astype(x.dtype)


def _layernorm(x, gain):
    xf = x.astype(jnp.float32)
    mu = jnp.mean(xf, axis=-1, keepdims=True)
    var = jnp.mean(jnp.square(xf - mu), axis=-1, keepdims=True)
    y = (xf - mu) * lax.rsqrt(var + EPS)
    return (y * gain.astype(jnp.float32)).astype(x.dtype)


def _modulate(h, shift, scale):
    return h * (1.0 + scale[:, None, :]) + shift[:, None, :]


def _rope_partial(x, positions, rot_dim):
    half = rot_dim // 2
    inv = ROPE_THETA ** (-2.0 * jnp.arange(half, dtype=jnp.float32) / rot_dim)
    ang = positions.astype(jnp.float32)[..., None] * inv
    cos = jnp.cos(ang)[:, :, None, :]
    sin = jnp.sin(ang)[:, :, None, :]
    xr = x[..., :rot_dim].astype(jnp.float32)
    x1, x2 = xr[..., :half], xr[..., half:]
    rot = jnp.concatenate([x1 * cos - x2 * sin, x2 * cos + x1 * sin], axis=-1)
    return jnp.concatenate([rot.astype(x.dtype), x[..., rot_dim:]], axis=-1)


def _swiglu(h, w1, w3, w2):
    return (jax.nn.silu(h @ w1) * (h @ w3)) @ w2


def _gmlp_mixer(u, v, v_gain, w_spatial, b_spatial):
    bsz, seq, _ = u.shape
    u = jax.nn.gelu(u)
    v = jax.nn.gelu(v)
    v = v.reshape(bsz, seq // GMLP_WIN, GMLP_WIN, A_GROUPS, A_GROUP_DIM)
    v = _layernorm(v, v_gain.reshape(A_GROUPS, A_GROUP_DIM))
    i = jnp.arange(GMLP_WIN)
    mask = (i[None, :] // CHUNK) <= (i[:, None] // CHUNK)
    w = jnp.where(mask[None], w_spatial, 0.0)
    mixed = jnp.einsum('gij,bnjgc->bnigc', w, v) + b_spatial.T[None, None, :, :, None]
    return u * mixed.reshape(bsz, seq, A_WIDTH)


def _dsa_mixer(q, k, v, q_idx, k_idx, w_idx, positions, q_norm_g, k_norm_g):
    bsz, seq, _ = q.shape
    q = q.reshape(bsz, seq, B_HEADS, B_HEAD_DIM)
    k = k.reshape(bsz, seq, B_HEADS, B_HEAD_DIM)
    v = v.reshape(bsz, seq, B_HEADS, B_HEAD_DIM)
    q = _rope_partial(_rms(q, q_norm_g), positions, B_ROT_DIM)
    k = _rope_partial(_rms(k, k_norm_g), positions, B_ROT_DIM)
    q_idx = _rope_partial(q_idx.reshape(bsz, seq, IDX_HEADS, IDX_HEAD_DIM), positions, IDX_ROT_DIM)
    k_idx = _rope_partial(k_idx[:, :, None, :], positions, IDX_ROT_DIM)[:, :, 0, :]
    k_idx32 = k_idx.astype(jnp.float32)

    topk = min(TOPK_MAX, seq // 4)
    nblk = seq // Q_BLOCK
    key_chunk = jnp.arange(seq) // CHUNK
    w_scale = (IDX_HEADS ** -0.5) * (IDX_HEAD_DIM ** -0.5)
    attn_scale = B_HEAD_DIM ** -0.5

    def to_blocks(a):
        return a.reshape(bsz, nblk, Q_BLOCK, *a.shape[2:]).swapaxes(0, 1)

    def gather_rows(kb, ib):
        return kb[ib]

    def one_block(args):
        start, qb, qib, wb = args
        q_chunk = (start + jnp.arange(Q_BLOCK)) // CHUNK
        admiss = key_chunk[None, :] <= q_chunk[:, None]
        logits = jnp.einsum('bqhd,bsd->bqhs', qib.astype(jnp.float32), k_idx32)
        iscore = jnp.einsum('bqh,bqhs->bqs', wb.astype(jnp.float32) * w_scale, jax.nn.relu(logits))
        iscore = jnp.where(admiss[None], iscore, NEG)
        _, sel = lax.top_k(iscore, topk)
        valid = key_chunk[sel] <= q_chunk[None, :, None]
        kg = jax.vmap(gather_rows)(k, sel)
        vg = jax.vmap(gather_rows)(v, sel)
        s = jnp.einsum('bqhd,bqkhd->bqhk', qb.astype(jnp.float32), kg.astype(jnp.float32)) * attn_scale
        s = jnp.where(valid[:, :, None, :], s, NEG)
        p = jax.nn.softmax(s, axis=-1)
        o = jnp.einsum('bqhk,bqkhd->bqhd', p, vg.astype(jnp.float32))
        return o.astype(qb.dtype)

    starts = jnp.arange(nblk) * Q_BLOCK
    out = lax.map(one_block, (starts, to_blocks(q), to_blocks(q_idx), to_blocks(w_idx)))
    return out.swapaxes(0, 1).reshape(bsz, seq, B_WIDTH)


def setup_inputs(seed: int = 0) -> dict:
    key = jax.random.key(seed)
    ks = jax.random.split(key, 24)
    f32 = jnp.float32

    def nrm(k, shape, fan_in):
        return jax.random.normal(k, shape, f32) * (fan_in ** -0.5)

    def gain(k, shape):
        return 1.0 + 0.01 * jax.random.normal(k, shape, f32)

    x = jax.random.normal(ks[0], (BATCH, SEQ, D_MODEL), f32)
    c = jax.random.normal(ks[1], (BATCH, D_MODEL), f32)
    start = jax.random.randint(ks[2], (BATCH, 1), 0, 1024, dtype=jnp.int32) * CHUNK
    positions = (start + jnp.arange(SEQ, dtype=jnp.int32)[None, :]).astype(jnp.int32)
    L = DEPTH
    return {
        "x": x,
        "c": c,
        "positions": positions,
        "w_ada": nrm(ks[3], (L, D_MODEL, 9 * D_MODEL), D_MODEL),
        "b_ada": 0.01 * jax.random.normal(ks[4], (L, 9 * D_MODEL), f32),
        "norm1_g": gain(ks[5], (L, D_MODEL)),
        "ffn1_w1": nrm(ks[6], (L, D_MODEL, D_FF), D_MODEL),
        "ffn1_w3": nrm(ks[7], (L, D_MODEL, D_FF), D_MODEL),
        "ffn1_w2": nrm(ks[8], (L, D_FF, D_MODEL), D_FF),
        "norm2_g": gain(ks[9], (L, D_MODEL)),
        "w_in": nrm(ks[10], (L, D_MODEL, N_IN_COLS), D_MODEL),
        "gmlp_v_g": gain(ks[11], (L, A_WIDTH)),
        "gmlp_ws": nrm(ks[12], (L, A_GROUPS, GMLP_WIN, GMLP_WIN), GMLP_WIN),
        "gmlp_b": gain(ks[13], (L, A_GROUPS, GMLP_WIN)),
        "q_norm_g": gain(ks[14], (L, B_HEAD_DIM)),
        "k_norm_g": gain(ks[15], (L, B_HEAD_DIM)),
        "out_norm_g": gain(ks[16], (L, MIX_WIDTH)),
        "w_out": nrm(ks[17], (L, MIX_WIDTH, D_MODEL), MIX_WIDTH),
        "norm3_g": gain(ks[18], (L, D_MODEL)),
        "ffn2_w1": nrm(ks[19], (L, D_MODEL, D_FF), D_MODEL),
        "ffn2_w3": nrm(ks[20], (L, D_MODEL, D_FF), D_MODEL),
        "ffn2_w2": nrm(ks[21], (L, D_FF, D_MODEL), D_FF),
    }


def reference(x, c, positions, w_ada, b_ada, norm1_g, ffn1_w1, ffn1_w3, ffn1_w2,
              norm2_g, w_in, gmlp_v_g, gmlp_ws, gmlp_b, q_norm_g, k_norm_g,
              out_norm_g, w_out, norm3_g, ffn2_w1, ffn2_w3, ffn2_w2):
    bsz, seq, _ = x.shape
    c_act = jax.nn.silu(c)
    for l in range(DEPTH):
        mod = c_act @ w_ada[l] + b_ada[l]
        sh1, sc1, g1, sh2, sc2, g2, sh3, sc3, g3 = jnp.split(mod, 9, axis=-1)

        h = _modulate(_rms(x, norm1_g[l]), sh1, sc1)
        x = x + 0.5 * g1[:, None, :] * _swiglu(h, ffn1_w1[l], ffn1_w3[l], ffn1_w2[l])

        h = _modulate(_rms(x, norm2_g[l]), sh2, sc2)
        proj = h @ w_in[l]
        a_u, a_v, b_q, b_k, b_v, i_q, i_k, i_w = jnp.split(proj, COL_SPLITS, axis=-1)
        out_a = _gmlp_mixer(a_u, a_v, gmlp_v_g[l], gmlp_ws[l], gmlp_b[l])
        out_b = _dsa_mixer(b_q, b_k, b_v, i_q, i_k, i_w, positions, q_norm_g[l], k_norm_g[l])
        groups = jnp.concatenate([
            out_a.reshape(bsz, seq, A_GROUPS, A_GROUP_DIM),
            out_b.reshape(bsz, seq, B_HEADS, B_HEAD_DIM)], axis=2)
        groups = _rms(groups, out_norm_g[l].reshape(A_GROUPS + B_HEADS, A_GROUP_DIM))
        mixed = groups.reshape(bsz, seq, MIX_WIDTH) @ w_out[l]
        x = x + g2[:, None, :] * mixed

        h = _modulate(_rms(x, norm3_g[l]), sh3, sc3)
        x = x + 0.5 * g3[:, None, :] * _swiglu(h, ffn2_w1[l], ffn2_w3[l], ffn2_w2[l])
    return x
```

```python
import functools

import jax
import jax.numpy as jnp
from jax import lax
from jax.experimental import pallas as pl
from jax.experimental.pallas import tpu as pltpu

F32 = jnp.float32
BF16 = jnp.bfloat16

CHUNK = 64
GMLP_WIN = 128
GROUP_DIM = 128
IDX_HEADS = 16
IDX_HEAD_DIM = 64
TOPK_MAX = 256
ROPE_THETA = 500000.0
B_ROT_DIM = GROUP_DIM // 4
IDX_ROT_DIM = IDX_HEAD_DIM // 4
EPS = 1e-6
NEG = -1e30

LANES = 128
INT_MIN = -(2 ** 31)
VMEM_LIMIT = 56 * 1024 * 1024


def _cparams(sem):
    return pltpu.CompilerParams(dimension_semantics=sem, vmem_limit_bytes=VMEM_LIMIT)


def _rms_mod(x, gain, shift, scale):
    ms = jnp.mean(x * x, axis=-1, keepdims=True)
    y = x * lax.rsqrt(ms + EPS) * gain
    return y * (1.0 + scale) + shift


def _rope_tables(pos_ref, ropec_ref):
    ang = pos_ref[...].astype(F32) * ropec_ref[0:1, :]
    cos = jnp.cos(ang)
    sin = jnp.sin(ang)
    return cos, sin * ropec_ref[1:2, :], sin * ropec_ref[2:3, :]


def _rope_apply(x, cos, sa, sb, half):
    return x * cos + pltpu.roll(x, half, 1) * sa + pltpu.roll(x, LANES - half, 1) * sb


def _ada_kernel(c_ref, w_ref, b_ref, o_ref):
    c = c_ref[...]
    act = c * jax.nn.sigmoid(c)
    o_ref[...] = jnp.dot(act, w_ref[...], preferred_element_type=F32) + b_ref[...]


def _ada(c_pad, w_ada, b_ada, tn=1024):
    rows, d = c_pad.shape
    n = w_ada.shape[1]
    return pl.pallas_call(
        _ada_kernel,
        grid=(n // tn,),
        in_specs=[pl.BlockSpec((rows, d), lambda j: (0, 0)),
                  pl.BlockSpec((d, tn), lambda j: (0, j)),
                  pl.BlockSpec((1, tn), lambda j: (0, j))],
        out_specs=pl.BlockSpec((rows, tn), lambda j: (0, j)),
        out_shape=jax.ShapeDtypeStruct((rows, n), F32),
        compiler_params=_cparams(("arbitrary",)),
        name="ada",
    )(c_pad, w_ada, b_ada)


def _ffn_kernel(x_ref, mod_ref, g_ref, w1_ref, w3_ref, w2_ref, o_ref, h_ref, *, row0):
    j = pl.program_id(1)

    @pl.when(j == 0)
    def _():
        h = _rms_mod(x_ref[...], g_ref[...], mod_ref[0, row0:row0 + 1, :], mod_ref[0, row0 + 1:row0 + 2, :])
        h_ref[...] = h.astype(BF16)
        o_ref[...] = jnp.zeros_like(o_ref)

    h = h_ref[...]
    a = jnp.dot(h, w1_ref[...], preferred_element_type=F32)
    b = jnp.dot(h, w3_ref[...], preferred_element_type=F32)
    act = (a * jax.nn.sigmoid(a) * b).astype(BF16)
    o_ref[...] += jnp.dot(act, w2_ref[...], preferred_element_type=F32)

    @pl.when(j == pl.num_programs(1) - 1)
    def _():
        o_ref[...] = x_ref[...] + (0.5 * mod_ref[0, row0 + 2:row0 + 3, :]) * o_ref[...]


def _ffn(x, mod, gain, w1, w3, w2, *, row0, seq, tm=512, tf=512):
    t, d = x.shape
    f = w1.shape[1]
    bpb = seq // tm
    return pl.pallas_call(
        functools.partial(_ffn_kernel, row0=row0),
        grid=(t // tm, f // tf),
        in_specs=[pl.BlockSpec((tm, d), lambda i, j: (i, 0)),
                  pl.BlockSpec((1,) + mod.shape[1:], lambda i, j: (i // bpb, 0, 0)),
                  pl.BlockSpec((1, d), lambda i, j: (0, 0)),
                  pl.BlockSpec((d, tf), lambda i, j: (0, j)),
                  pl.BlockSpec((d, tf), lambda i, j: (0, j)),
                  pl.BlockSpec((tf, d), lambda i, j: (j, 0))],
        out_specs=pl.BlockSpec((tm, d), lambda i, j: (i, 0)),
        out_shape=jax.ShapeDtypeStruct((t, d), F32),
        scratch_shapes=[pltpu.VMEM((tm, d), BF16)],
        compiler_params=_cparams(("parallel", "arbitrary")),
        name="ffn",
    )(x, mod, gain, w1, w3, w2)


def _gmlp_kernel(x_ref, mod_ref, g_ref, wu_ref, wv_ref, vg_ref, ws_ref, bs_ref, og_ref, o_ref, h_ref, *, row0):
    n = pl.program_id(1)

    @pl.when(n == 0)
    def _():
        h = _rms_mod(x_ref[...], g_ref[...], mod_ref[0, row0:row0 + 1, :], mod_ref[0, row0 + 1:row0 + 2, :])
        h_ref[...] = h.astype(BF16)

    h = h_ref[...]
    u = jax.nn.gelu(jnp.dot(h, wu_ref[...], preferred_element_type=F32))
    v = jax.nn.gelu(jnp.dot(h, wv_ref[...], preferred_element_type=F32))
    tm = u.shape[0]
    groups = u.shape[1] // GROUP_DIM
    ri = lax.broadcasted_iota(jnp.int32, (GMLP_WIN, GMLP_WIN), 0) // CHUNK
    ci = lax.broadcasted_iota(jnp.int32, (GMLP_WIN, GMLP_WIN), 1) // CHUNK
    for g in range(groups):
        lo = g * GROUP_DIM
        vg = v[:, lo:lo + GROUP_DIM]
        mu = jnp.mean(vg, axis=-1, keepdims=True)
        vc = vg - mu
        var = jnp.mean(vc * vc, axis=-1, keepdims=True)
        y = (vc * lax.rsqrt(var + EPS) * vg_ref[:, lo:lo + GROUP_DIM]).astype(BF16)
        w = jnp.where(ci <= ri, ws_ref[g], 0.0).astype(BF16)
        for win in range(tm // GMLP_WIN):
            r0 = win * GMLP_WIN
            mixed = jnp.dot(w, y[r0:r0 + GMLP_WIN], preferred_element_type=F32) + bs_ref[g]
            o = u[r0:r0 + GMLP_WIN, lo:lo + GROUP_DIM] * mixed
            ms = jnp.mean(o * o, axis=-1, keepdims=True)
            o = o * lax.rsqrt(ms + EPS) * og_ref[:, lo:lo + GROUP_DIM]
            o_ref[r0:r0 + GMLP_WIN, lo:lo + GROUP_DIM] = o.astype(o_ref.dtype)


def _gmlp(x, mod, gain, wu, wv, v_gain, ws, bs, out_gain, *, row0, seq, tm=512, tn=256):
    t, d = x.shape
    aw = wu.shape[1]
    bpb = seq // tm
    gpt = tn // GROUP_DIM
    return pl.pallas_call(
        functools.partial(_gmlp_kernel, row0=row0),
        grid=(t // tm, aw // tn),
        in_specs=[pl.BlockSpec((tm, d), lambda i, n: (i, 0)),
                  pl.BlockSpec((1,) + mod.shape[1:], lambda i, n: (i // bpb, 0, 0)),
                  pl.BlockSpec((1, d), lambda i, n: (0, 0)),
                  pl.BlockSpec((d, tn), lambda i, n: (0, n)),
                  pl.BlockSpec((d, tn), lambda i, n: (0, n)),
                  pl.BlockSpec((1, tn), lambda i, n: (0, n)),
                  pl.BlockSpec((gpt, GMLP_WIN, GMLP_WIN), lambda i, n: (n, 0, 0)),
                  pl.BlockSpec((gpt, GMLP_WIN, GROUP_DIM), lambda i, n: (n, 0, 0)),
                  pl.BlockSpec((1, tn), lambda i, n: (0, n))],
        out_specs=pl.BlockSpec((tm, tn), lambda i, n: (i, n)),
        out_shape=jax.ShapeDtypeStruct((t, aw), BF16),
        scratch_shapes=[pltpu.VMEM((tm, d), BF16)],
        compiler_params=_cparams(("parallel", "arbitrary")),
        name="gmlp",
    )(x, mod, gain, wu, wv, v_gain, ws, bs, out_gain)


def _qkv_kernel(x_ref, mod_ref, g_ref, pos_ref, ropec_ref, w_ref, qkg_ref, o_ref, h_ref, cos_ref, sa_ref, sb_ref,
                *, row0, qk_tiles, q_tiles, attn_scale):
    n = pl.program_id(1)

    @pl.when(n == 0)
    def _():
        h = _rms_mod(x_ref[...], g_ref[...], mod_ref[0, row0:row0 + 1, :], mod_ref[0, row0 + 1:row0 + 2, :])
        h_ref[...] = h.astype(BF16)
        cos, sa, sb = _rope_tables(pos_ref, ropec_ref)
        cos_ref[...] = cos
        sa_ref[...] = sa
        sb_ref[...] = sb

    acc = jnp.dot(h_ref[...], w_ref[...], preferred_element_type=F32)
    heads = acc.shape[1] // GROUP_DIM

    @pl.when(n < qk_tiles)
    def _():
        is_q = n < q_tiles
        gain = jnp.where(is_q, qkg_ref[0:1, :], qkg_ref[1:2, :])
        mult = jnp.where(is_q, attn_scale, 1.0)
        for hd in range(heads):
            lo = hd * GROUP_DIM
            xh = acc[:, lo:lo + GROUP_DIM]
            ms = jnp.mean(xh * xh, axis=-1, keepdims=True)
            xh = xh * lax.rsqrt(ms + EPS) * gain
            xh = _rope_apply(xh, cos_ref[...], sa_ref[...], sb_ref[...], B_ROT_DIM // 2)
            o_ref[:, lo:lo + GROUP_DIM] = (xh * mult).astype(o_ref.dtype)

    @pl.when(n >= qk_tiles)
    def _():
        o_ref[...] = acc.astype(o_ref.dtype)


def _qkv(x, mod, gain, pos, ropec, w, qk_gain, *, row0, seq, tm=512, tn=512):
    t, d = x.shape
    ncol = w.shape[1]
    bw = ncol // 3
    bpb = seq // tm
    kern = functools.partial(_qkv_kernel, row0=row0, qk_tiles=2 * bw // tn, q_tiles=bw // tn,
                             attn_scale=float(GROUP_DIM) ** -0.5)
    return pl.pallas_call(
        kern,
        grid=(t // tm, ncol // tn),
        in_specs=[pl.BlockSpec((tm, d), lambda i, n: (i, 0)),
                  pl.BlockSpec((1,) + mod.shape[1:], lambda i, n: (i // bpb, 0, 0)),
                  pl.BlockSpec((1, d), lambda i, n: (0, 0)),
                  pl.BlockSpec((tm, 1), lambda i, n: (i, 0)),
                  pl.BlockSpec(ropec.shape, lambda i, n: (0, 0)),
                  pl.BlockSpec((d, tn), lambda i, n: (0, n)),
                  pl.BlockSpec(qk_gain.shape, lambda i, n: (0, 0))],
        out_specs=pl.BlockSpec((tm, tn), lambda i, n: (i, n)),
        out_shape=jax.ShapeDtypeStruct((t, ncol), BF16),
        scratch_shapes=[pltpu.VMEM((tm, d), BF16)] + [pltpu.VMEM((tm, LANES), F32)] * 3,
        compiler_params=_cparams(("parallel", "arbitrary")),
        name="qkv",
    )(x, mod, gain, pos, ropec, w, qk_gain)


def _idx_kernel(x_ref, mod_ref, g_ref, pos_ref, ropec_ref, w_ref, qi_ref, tail_ref, kit_ref,
                h_ref, cos_ref, sa_ref, sb_ref, *, row0, q_tiles, w_scale):
    n = pl.program_id(1)

    @pl.when(n == 0)
    def _():
        h = _rms_mod(x_ref[...], g_ref[...], mod_ref[0, row0:row0 + 1, :], mod_ref[0, row0 + 1:row0 + 2, :])
        h_ref[...] = h.astype(BF16)
        cos, sa, sb = _rope_tables(pos_ref, ropec_ref)
        cos_ref[...] = cos
        sa_ref[...] = sa
        sb_ref[...] = sb

    acc = jnp.dot(h_ref[...], w_ref[...], preferred_element_type=F32)
    half = IDX_ROT_DIM // 2

    @pl.when(n < q_tiles)
    def _():
        for c in range(acc.shape[1] // LANES):
            xr = _rope_apply(acc[:, c * LANES:(c + 1) * LANES], cos_ref[...], sa_ref[...], sb_ref[...], half)
            for sub in range(LANES // IDX_HEAD_DIM):
                hd = c * (LANES // IDX_HEAD_DIM) + sub
                qi_ref[hd] = xr[:, sub * IDX_HEAD_DIM:(sub + 1) * IDX_HEAD_DIM].astype(qi_ref.dtype)

    @pl.when(n == q_tiles)
    def _():
        lane = lax.broadcasted_iota(jnp.int32, (1, LANES), 1)
        x = acc[:, 0:LANES]
        xr = _rope_apply(x, cos_ref[...], sa_ref[...], sb_ref[...], half)
        tl = jnp.where(lane < IDX_HEAD_DIM, xr, x * w_scale)
        tail_ref[...] = tl
        kit_ref[...] = jnp.transpose(tl)[0:IDX_HEAD_DIM, :].astype(kit_ref.dtype)


def _idx(x, mod, gain, pos, ropec, w, *, row0, seq, tm=512, tn=256):
    t, d = x.shape
    ncol = w.shape[1]
    bpb = seq // tm
    q_tiles = (IDX_HEADS * IDX_HEAD_DIM) // tn
    hpt = tn // IDX_HEAD_DIM
    kern = functools.partial(_idx_kernel, row0=row0, q_tiles=q_tiles,
                             w_scale=float(IDX_HEADS) ** -0.5 * float(IDX_HEAD_DIM) ** -0.5)
    return pl.pallas_call(
        kern,
        grid=(t // tm, ncol // tn),
        in_specs=[pl.BlockSpec((tm, d), lambda i, n: (i, 0)),
                  pl.BlockSpec((1,) + mod.shape[1:], lambda i, n: (i // bpb, 0, 0)),
                  pl.BlockSpec((1, d), lambda i, n: (0, 0)),
                  pl.BlockSpec((tm, 1), lambda i, n: (i, 0)),
                  pl.BlockSpec(ropec.shape, lambda i, n: (0, 0)),
                  pl.BlockSpec((d, tn), lambda i, n: (0, n))],
        out_specs=[pl.BlockSpec((hpt, tm, IDX_HEAD_DIM), lambda i, n: (jnp.minimum(n, q_tiles - 1), i, 0)),
                   pl.BlockSpec((tm, LANES), lambda i, n: (i, 0)),
                   pl.BlockSpec((IDX_HEAD_DIM, tm), lambda i, n: (0, i))],
        out_shape=[jax.ShapeDtypeStruct((IDX_HEADS, t, IDX_HEAD_DIM), BF16),
                   jax.ShapeDtypeStruct((t, LANES), F32),
                   jax.ShapeDtypeStruct((IDX_HEAD_DIM, t), BF16)],
        scratch_shapes=[pltpu.VMEM((tm, d), BF16)] + [pltpu.VMEM((tm, LANES), F32)] * 3,
        compiler_params=_cparams(("parallel", "arbitrary")),
        name="idx",
    )(x, mod, gain, pos, ropec, w)


def _dsa_kernel(q_ref, k_ref, v_ref, qi_ref, kit_ref, w_ref, og_ref, o_ref, sc_ref, key_ref, *, topk, tk):
    tq, seq = sc_ref.shape
    jq = pl.program_id(1)
    heads = q_ref.shape[1] // GROUP_DIM

    q_chunk = (jq * tq + lax.broadcasted_iota(jnp.int32, (tq, 1), 0)) // CHUNK
    for kt in range(seq // tk):
        kit = kit_ref[:, kt * tk:(kt + 1) * tk]
        acc = jnp.zeros((tq, tk), F32)
        for hd in range(IDX_HEADS):
            logit = jnp.dot(qi_ref[hd], kit, preferred_element_type=F32)
            acc = acc + jnp.maximum(logit, 0.0) * w_ref[:, IDX_HEAD_DIM + hd:IDX_HEAD_DIM + hd + 1]
        k_chunk = (kt * tk + lax.broadcasted_iota(jnp.int32, (1, tk), 1)) // CHUNK
        acc = jnp.where(k_chunk <= q_chunk, acc, NEG)
        sc_ref[:, kt * tk:(kt + 1) * tk] = acc
        bits = pltpu.bitcast(acc, jnp.int32)
        key_ref[:, kt * tk:(kt + 1) * tk] = bits ^ ((bits >> 31) & 0x7FFFFFFF)

    def bit_step(i, thr):
        cand = thr ^ (jnp.int32(1) << (31 - i))
        cnt = jnp.sum(jnp.where(key_ref[...] >= cand, 1.0, 0.0), axis=-1, keepdims=True)
        return jnp.where(cnt >= float(topk), cand, thr)

    thr = lax.fori_loop(0, 32, bit_step, jnp.full((tq, 1), INT_MIN, jnp.int32))

    sel = (key_ref[...] >= thr) & (sc_ref[...] > NEG)
    sc_ref[...] = jnp.where(sel, 0.0, NEG)

    for hd in range(heads):
        lo = hd * GROUP_DIM
        s = lax.dot_general(q_ref[:, lo:lo + GROUP_DIM], k_ref[:, lo:lo + GROUP_DIM],
                            (((1,), (1,)), ((), ())), preferred_element_type=F32)
        s = s + sc_ref[...]
        m = jnp.max(s, axis=-1, keepdims=True)
        p = jnp.exp(s - m)
        denom = jnp.sum(p, axis=-1, keepdims=True)
        o = jnp.dot(p.astype(BF16), v_ref[:, lo:lo + GROUP_DIM], preferred_element_type=F32) / denom
        ms = jnp.mean(o * o, axis=-1, keepdims=True)
        o = o * lax.rsqrt(ms + EPS) * og_ref[:, lo:lo + GROUP_DIM]
        o_ref[:, lo:lo + GROUP_DIM] = o.astype(o_ref.dtype)


def _dsa(qkv, qi, kit, tail, out_gain, *, bsz, seq, tq=256, tk=512):
    t = qkv.shape[0]
    bw = qkv.shape[1] // 3
    nq = seq // tq
    topk = min(TOPK_MAX, seq // 4)
    once = pl.Buffered(1)
    return pl.pallas_call(
        functools.partial(_dsa_kernel, topk=topk, tk=tk),
        grid=(bsz, nq),
        in_specs=[pl.BlockSpec((tq, bw), lambda b, j: (b * nq + j, 0)),
                  pl.BlockSpec((seq, bw), lambda b, j: (b, 1), pipeline_mode=once),
                  pl.BlockSpec((seq, bw), lambda b, j: (b, 2), pipeline_mode=once),
                  pl.BlockSpec((IDX_HEADS, tq, IDX_HEAD_DIM), lambda b, j: (0, b * nq + j, 0)),
                  pl.BlockSpec((IDX_HEAD_DIM, seq), lambda b, j: (0, b), pipeline_mode=once),
                  pl.BlockSpec((tq, LANES), lambda b, j: (b * nq + j, 0)),
                  pl.BlockSpec((1, bw), lambda b, j: (0, 0))],
        out_specs=pl.BlockSpec((tq, bw), lambda b, j: (b * nq + j, 0)),
        out_shape=jax.ShapeDtypeStruct((t, bw), BF16),
        scratch_shapes=[pltpu.VMEM((tq, seq), F32), pltpu.VMEM((tq, seq), jnp.int32)],
        compiler_params=_cparams(("parallel", "arbitrary")),
        name="dsa",
    )(qkv, qkv, qkv, qi, kit, tail, out_gain)


def _out_kernel(x_ref, mod_ref, a_ref, b_ref, wa_ref, wb_ref, o_ref, *, row0):
    mixed = jnp.dot(a_ref[...], wa_ref[...], preferred_element_type=F32)
    mixed = mixed + jnp.dot(b_ref[...], wb_ref[...], preferred_element_type=F32)
    o_ref[...] = x_ref[...] + mod_ref[0, row0:row0 + 1, :] * mixed


def _out(x, mod, ga, gb, w_out, *, row0, seq, tm=512, tn=1024):
    t, d = x.shape
    half = ga.shape[1]
    bpb = seq // tm
    return pl.pallas_call(
        functools.partial(_out_kernel, row0=row0),
        grid=(t // tm, d // tn),
        in_specs=[pl.BlockSpec((tm, tn), lambda i, n: (i, n)),
                  pl.BlockSpec((1, mod.shape[1], tn), lambda i, n: (i // bpb, 0, n)),
                  pl.BlockSpec((tm, half), lambda i, n: (i, 0)),
                  pl.BlockSpec((tm, half), lambda i, n: (i, 0)),
                  pl.BlockSpec((half, tn), lambda i, n: (0, n)),
                  pl.BlockSpec((half, tn), lambda i, n: (1, n))],
        out_specs=pl.BlockSpec((tm, tn), lambda i, n: (i, n)),
        out_shape=jax.ShapeDtypeStruct((t, d), F32),
        compiler_params=_cparams(("parallel", "arbitrary")),
        name="out",
    )(x, mod, ga, gb, w_out, w_out)


def _rope_consts(rot_dim, period):
    half = rot_dim // 2
    inv = ROPE_THETA ** (-2.0 * jnp.arange(half, dtype=F32) / rot_dim)
    lane = jnp.arange(LANES) % period
    inv_l = jnp.where(lane < rot_dim, inv[lane % half], 0.0)
    sa = jnp.where((lane >= half) & (lane < rot_dim), 1.0, 0.0)
    sb = jnp.where(lane < half, -1.0, 0.0)
    return jnp.zeros((8, LANES), F32).at[0].set(inv_l).at[1].set(sa).at[2].set(sb)


def kernel(x, c, positions, w_ada, b_ada, norm1_g, ffn1_w1, ffn1_w3, ffn1_w2, norm2_g, w_in, gmlp_v_g, gmlp_ws,
           gmlp_b, q_norm_g, k_norm_g, out_norm_g, w_out, norm3_g, ffn2_w1, ffn2_w3, ffn2_w2):
    bsz, seq, d = x.shape
    depth = w_ada.shape[0]
    aw = gmlp_v_g.shape[1]
    bw = d - aw
    t = bsz * seq

    xf = x.reshape(t, d)
    pos = positions.reshape(t, 1)
    c_pad = jnp.zeros((8, d), F32).at[:bsz].set(c)
    rope_b = _rope_consts(B_ROT_DIM, GROUP_DIM)
    rope_i = _rope_consts(IDX_ROT_DIM, IDX_HEAD_DIM)
    idx_cols = IDX_HEADS * IDX_HEAD_DIM + IDX_HEAD_DIM + IDX_HEADS
    idx_pad = -idx_cols % 256

    for l in range(depth):
        mod = _ada(c_pad, w_ada[l], b_ada[l][None, :])[:bsz].reshape(bsz, 9, d)

        xf = _ffn(xf, mod, norm1_g[l][None, :], ffn1_w1[l].astype(BF16), ffn1_w3[l].astype(BF16),
                  ffn1_w2[l].astype(BF16), row0=0, seq=seq)

        w_l = w_in[l]
        wu = w_l[:, :aw].astype(BF16)
        wv = w_l[:, aw:2 * aw].astype(BF16)
        wqkv = w_l[:, 2 * aw:2 * aw + 3 * bw].astype(BF16)
        widx = jnp.pad(w_l[:, 2 * aw + 3 * bw:], ((0, 0), (0, idx_pad))).astype(BF16)
        g2 = norm2_g[l][None, :]
        bs = jnp.broadcast_to(gmlp_b[l][:, :, None], gmlp_b.shape[1:] + (GROUP_DIM,))

        ga = _gmlp(xf, mod, g2, wu, wv, gmlp_v_g[l][None, :], gmlp_ws[l], bs, out_norm_g[l][None, :aw],
                   row0=3, seq=seq)
        qkv = _qkv(xf, mod, g2, pos, rope_b, wqkv, jnp.stack([q_norm_g[l], k_norm_g[l]]), row0=3, seq=seq)
        qi, tail, kit = _idx(xf, mod, g2, pos, rope_i, widx, row0=3, seq=seq)
        gb = _dsa(qkv, qi, kit, tail, out_norm_g[l][None, aw:], bsz=bsz, seq=seq)
        xf = _out(xf, mod, ga, gb, w_out[l].astype(BF16), row0=5, seq=seq)

        xf = _ffn(xf, mod, norm3_g[l][None, :], ffn2_w1[l].astype(BF16), ffn2_w3[l].astype(BF16),
                  ffn2_w2[l].astype(BF16), row0=6, seq=seq)
    return xf.reshape(bsz, seq, d)
```

```python
import functools

import jax
import jax.numpy as jnp
from jax import lax
from jax.experimental import pallas as pl
from jax.experimental.pallas import tpu as pltpu

F32 = jnp.float32
BF16 = jnp.bfloat16

CHUNK = 64
GMLP_WIN = 128
GROUP_DIM = 128
IDX_HEADS = 16
IDX_HEAD_DIM = 64
TOPK_MAX = 256
ROPE_THETA = 500000.0
B_ROT_DIM = GROUP_DIM // 4
IDX_ROT_DIM = IDX_HEAD_DIM // 4
EPS = 1e-6
NEG = -1e30

LANES = 128
I16_ROWS = 16
LOG2E = 1.4426950408889634
PROJ_TN = 256
DSA_TILE = 256
SCORE_ROWS = 128
COUNT_CHAINS = 4
VMEM_LIMIT = 56 * 1024 * 1024


def _cparams(sem):
    return pltpu.CompilerParams(dimension_semantics=sem, vmem_limit_bytes=VMEM_LIMIT)


def _rms_mod(x, gain, shift, scale):
    ms = jnp.mean(x * x, axis=-1, keepdims=True)
    y = x * lax.rsqrt(ms + EPS) * gain
    return y * (1.0 + scale) + shift


def _rope_tables(pos_ref, ropec_ref):
    ang = pos_ref[...].astype(F32) * ropec_ref[0:1, :]
    cos = jnp.cos(ang)
    sin = jnp.sin(ang)
    return cos, sin * ropec_ref[1:2, :], sin * ropec_ref[2:3, :]


def _rope_apply(x, cos, sa, sb, half):
    return x * cos + pltpu.roll(x, half, 1) * sa + pltpu.roll(x, LANES - half, 1) * sb


def _ada_kernel(c_ref, w_ref, b_ref, o_ref):
    c = c_ref[...]
    act = c * jax.nn.sigmoid(c)
    o_ref[...] = jnp.dot(act, w_ref[...], preferred_element_type=F32) + b_ref[...]


def _ada(c_pad, w_ada, b_ada, tn=1024):
    rows, d = c_pad.shape
    n = w_ada.shape[1]
    return pl.pallas_call(
        _ada_kernel,
        grid=(n // tn,),
        in_specs=[pl.BlockSpec((rows, d), lambda j: (0, 0)),
                  pl.BlockSpec((d, tn), lambda j: (0, j)),
                  pl.BlockSpec((1, tn), lambda j: (0, j))],
        out_specs=pl.BlockSpec((rows, tn), lambda j: (0, j)),
        out_shape=jax.ShapeDtypeStruct((rows, n), F32),
        compiler_params=_cparams(("arbitrary",)),
        name="ada",
    )(c_pad, w_ada, b_ada)


def _ffn_kernel(x_ref, mod_ref, g_ref, w1_ref, w3_ref, w2_ref, o_ref, h_ref, *, row0):
    j = pl.program_id(1)

    @pl.when(j == 0)
    def _():
        h = _rms_mod(x_ref[...], g_ref[...], mod_ref[0, row0:row0 + 1, :], mod_ref[0, row0 + 1:row0 + 2, :])
        h_ref[...] = h.astype(BF16)
        o_ref[...] = jnp.zeros_like(o_ref)

    h = h_ref[...]
    a = jnp.dot(h, w1_ref[...], preferred_element_type=F32)
    b = jnp.dot(h, w3_ref[...], preferred_element_type=F32)
    act = (a * jax.nn.sigmoid(a) * b).astype(BF16)
    o_ref[...] += jnp.dot(act, w2_ref[...], preferred_element_type=F32)

    @pl.when(j == pl.num_programs(1) - 1)
    def _():
        o_ref[...] = x_ref[...] + (0.5 * mod_ref[0, row0 + 2:row0 + 3, :]) * o_ref[...]


def _ffn(x, mod, gain, w1, w3, w2, *, row0, seq, tm=512, tf=512):
    t, d = x.shape
    f = w1.shape[1]
    bpb = seq // tm
    return pl.pallas_call(
        functools.partial(_ffn_kernel, row0=row0),
        grid=(t // tm, f // tf),
        in_specs=[pl.BlockSpec((tm, d), lambda i, j: (i, 0)),
                  pl.BlockSpec((1,) + mod.shape[1:], lambda i, j: (i // bpb, 0, 0)),
                  pl.BlockSpec((1, d), lambda i, j: (0, 0)),
                  pl.BlockSpec((d, tf), lambda i, j: (0, j)),
                  pl.BlockSpec((d, tf), lambda i, j: (0, j)),
                  pl.BlockSpec((tf, d), lambda i, j: (j, 0))],
        out_specs=pl.BlockSpec((tm, d), lambda i, j: (i, 0)),
        out_shape=jax.ShapeDtypeStruct((t, d), F32),
        scratch_shapes=[pltpu.VMEM((tm, d), BF16)],
        compiler_params=_cparams(("parallel", "arbitrary")),
        name="ffn",
    )(x, mod, gain, w1, w3, w2)


def _gmlp_kernel(x_ref, mod_ref, g_ref, wu_ref, wv_ref, vg_ref, ws_ref, bs_ref, og_ref, o_ref, h_ref, *, row0):
    h = _rms_mod(x_ref[...], g_ref[...], mod_ref[0, row0:row0 + 1, :], mod_ref[0, row0 + 1:row0 + 2, :])
    h_ref[...] = h.astype(BF16)
    tm = h_ref.shape[0]
    ri = lax.broadcasted_iota(jnp.int32, (GMLP_WIN, GMLP_WIN), 0) // CHUNK
    ci = lax.broadcasted_iota(jnp.int32, (GMLP_WIN, GMLP_WIN), 1) // CHUNK
    for c0 in range(0, wu_ref.shape[1], PROJ_TN):
        u = jax.nn.gelu(jnp.dot(h_ref[...], wu_ref[:, c0:c0 + PROJ_TN], preferred_element_type=F32))
        v = jax.nn.gelu(jnp.dot(h_ref[...], wv_ref[:, c0:c0 + PROJ_TN], preferred_element_type=F32))
        for sub in range(PROJ_TN // GROUP_DIM):
            lo = c0 + sub * GROUP_DIM
            g = lo // GROUP_DIM
            vg = v[:, sub * GROUP_DIM:(sub + 1) * GROUP_DIM]
            mu = jnp.mean(vg, axis=-1, keepdims=True)
            vc = vg - mu
            var = jnp.mean(vc * vc, axis=-1, keepdims=True)
            y = (vc * lax.rsqrt(var + EPS) * vg_ref[:, lo:lo + GROUP_DIM]).astype(BF16)
            w = jnp.where(ci <= ri, ws_ref[g], 0.0).astype(BF16)
            for win in range(tm // GMLP_WIN):
                r0 = win * GMLP_WIN
                mixed = jnp.dot(w, y[r0:r0 + GMLP_WIN], preferred_element_type=F32) + bs_ref[g]
                o = u[r0:r0 + GMLP_WIN, sub * GROUP_DIM:(sub + 1) * GROUP_DIM] * mixed
                ms = jnp.mean(o * o, axis=-1, keepdims=True)
                o = o * lax.rsqrt(ms + EPS) * og_ref[:, lo:lo + GROUP_DIM]
                o_ref[r0:r0 + GMLP_WIN, lo:lo + GROUP_DIM] = o.astype(o_ref.dtype)


def _gmlp(x, mod, gain, wu, wv, v_gain, ws, bs, out_gain, *, row0, seq, tm=512):
    t, d = x.shape
    aw = wu.shape[1]
    bpb = seq // tm
    once = pl.Buffered(1)
    return pl.pallas_call(
        functools.partial(_gmlp_kernel, row0=row0),
        grid=(t // tm,),
        in_specs=[pl.BlockSpec((tm, d), lambda i: (i, 0)),
                  pl.BlockSpec((1,) + mod.shape[1:], lambda i: (i // bpb, 0, 0)),
                  pl.BlockSpec((1, d), lambda i: (0, 0)),
                  pl.BlockSpec(wu.shape, lambda i: (0, 0), pipeline_mode=once),
                  pl.BlockSpec(wv.shape, lambda i: (0, 0), pipeline_mode=once),
                  pl.BlockSpec((1, aw), lambda i: (0, 0)),
                  pl.BlockSpec(ws.shape, lambda i: (0, 0, 0), pipeline_mode=once),
                  pl.BlockSpec(bs.shape, lambda i: (0, 0, 0), pipeline_mode=once),
                  pl.BlockSpec((1, aw), lambda i: (0, 0))],
        out_specs=pl.BlockSpec((tm, aw), lambda i: (i, 0)),
        out_shape=jax.ShapeDtypeStruct((t, aw), BF16),
        scratch_shapes=[pltpu.VMEM((tm, d), BF16)],
        compiler_params=_cparams(("parallel",)),
        name="gmlp",
    )(x, mod, gain, wu, wv, v_gain, ws, bs, out_gain)


def _qkv_kernel(x_ref, mod_ref, g_ref, pos_ref, ropec_ref, w_ref, qkg_ref, qt_ref, k_ref, vt_ref, h_ref,
                *, row0, attn_scale):
    h = _rms_mod(x_ref[...], g_ref[...], mod_ref[0, row0:row0 + 1, :], mod_ref[0, row0 + 1:row0 + 2, :])
    h_ref[...] = h.astype(BF16)
    cos, sa, sb = _rope_tables(pos_ref, ropec_ref)
    bw = k_ref.shape[1]

    def normed(xh, gain):
        ms = jnp.mean(xh * xh, axis=-1, keepdims=True)
        return _rope_apply(xh * lax.rsqrt(ms + EPS) * gain, cos, sa, sb, B_ROT_DIM // 2)

    for c0 in range(0, w_ref.shape[1], PROJ_TN):
        acc = jnp.dot(h_ref[...], w_ref[:, c0:c0 + PROJ_TN], preferred_element_type=F32)
        for sub in range(PROJ_TN // GROUP_DIM):
            xh = acc[:, sub * GROUP_DIM:(sub + 1) * GROUP_DIM]
            col = c0 + sub * GROUP_DIM
            if col < bw:
                qh = normed(xh, qkg_ref[0:1, :]) * attn_scale
                qt_ref[col:col + GROUP_DIM, :] = jnp.transpose(qh).astype(qt_ref.dtype)
            elif col < 2 * bw:
                k_ref[:, col - bw:col - bw + GROUP_DIM] = normed(xh, qkg_ref[1:2, :]).astype(k_ref.dtype)
            else:
                vh = jnp.transpose(xh).astype(vt_ref.dtype)
                for kb in range(vt_ref.shape[0]):
                    vt_ref[kb, col - 2 * bw:col - 2 * bw + GROUP_DIM, :] = vh[:, kb * DSA_TILE:(kb + 1) * DSA_TILE]


def _qkv(x, mod, gain, pos, ropec, w, qk_gain, *, row0, seq, tm=512):
    t, d = x.shape
    bw = w.shape[1] // 3
    bpb = seq // tm
    kern = functools.partial(_qkv_kernel, row0=row0, attn_scale=float(GROUP_DIM) ** -0.5 * LOG2E)
    return pl.pallas_call(
        kern,
        grid=(t // tm,),
        in_specs=[pl.BlockSpec((tm, d), lambda i: (i, 0)),
                  pl.BlockSpec((1,) + mod.shape[1:], lambda i: (i // bpb, 0, 0)),
                  pl.BlockSpec((1, d), lambda i: (0, 0)),
                  pl.BlockSpec((tm, 1), lambda i: (i, 0)),
                  pl.BlockSpec(ropec.shape, lambda i: (0, 0)),
                  pl.BlockSpec(w.shape, lambda i: (0, 0), pipeline_mode=pl.Buffered(1)),
                  pl.BlockSpec(qk_gain.shape, lambda i: (0, 0))],
        out_specs=[pl.BlockSpec((bw, tm), lambda i: (0, i)),
                   pl.BlockSpec((tm, bw), lambda i: (i, 0)),
                   pl.BlockSpec((tm // DSA_TILE, bw, DSA_TILE), lambda i: (i, 0, 0))],
        out_shape=[jax.ShapeDtypeStruct((bw, t), BF16),
                   jax.ShapeDtypeStruct((t, bw), BF16),
                   jax.ShapeDtypeStruct((t // DSA_TILE, bw, DSA_TILE), BF16)],
        scratch_shapes=[pltpu.VMEM((tm, d), BF16)],
        compiler_params=_cparams(("parallel",)),
        name="qkv",
    )(x, mod, gain, pos, ropec, w, qk_gain)


def _idx_kernel(x_ref, mod_ref, g_ref, pos_ref, ropec_ref, w_ref, qit_ref, ki_ref, wt_ref,
                h_ref, *, row0, w_scale):
    h = _rms_mod(x_ref[...], g_ref[...], mod_ref[0, row0:row0 + 1, :], mod_ref[0, row0 + 1:row0 + 2, :])
    h_ref[...] = h.astype(BF16)
    cos, sa, sb = _rope_tables(pos_ref, ropec_ref)
    half = IDX_ROT_DIM // 2
    q_cols = IDX_HEADS * IDX_HEAD_DIM

    for c0 in range(0, w_ref.shape[1], PROJ_TN):
        acc = jnp.dot(h_ref[...], w_ref[:, c0:c0 + PROJ_TN], preferred_element_type=F32)
        if c0 < q_cols:
            for c in range(PROJ_TN // LANES):
                xt = jnp.transpose(_rope_apply(acc[:, c * LANES:(c + 1) * LANES], cos, sa, sb, half))
                for sub in range(LANES // IDX_HEAD_DIM):
                    hd = (c0 + c * LANES) // IDX_HEAD_DIM + sub
                    qit_ref[hd] = xt[sub * IDX_HEAD_DIM:(sub + 1) * IDX_HEAD_DIM, :].astype(qit_ref.dtype)
        else:
            x = acc[:, 0:LANES]
            ki_ref[...] = _rope_apply(x, cos, sa, sb, half)[:, 0:IDX_HEAD_DIM].astype(ki_ref.dtype)
            wt_ref[...] = jnp.transpose(x * w_scale)[IDX_HEAD_DIM:IDX_HEAD_DIM + IDX_HEADS, :]


def _idx(x, mod, gain, pos, ropec, w, *, row0, seq, tm=512):
    t, d = x.shape
    bpb = seq // tm
    kern = functools.partial(_idx_kernel, row0=row0,
                             w_scale=float(IDX_HEADS) ** -0.5 * float(IDX_HEAD_DIM) ** -0.5)
    return pl.pallas_call(
        kern,
        grid=(t // tm,),
        in_specs=[pl.BlockSpec((tm, d), lambda i: (i, 0)),
                  pl.BlockSpec((1,) + mod.shape[1:], lambda i: (i // bpb, 0, 0)),
                  pl.BlockSpec((1, d), lambda i: (0, 0)),
                  pl.BlockSpec((tm, 1), lambda i: (i, 0)),
                  pl.BlockSpec(ropec.shape, lambda i: (0, 0)),
                  pl.BlockSpec(w.shape, lambda i: (0, 0), pipeline_mode=pl.Buffered(1))],
        out_specs=[pl.BlockSpec((IDX_HEADS, IDX_HEAD_DIM, tm), lambda i: (0, 0, i)),
                   pl.BlockSpec((tm, IDX_HEAD_DIM), lambda i: (i, 0)),
                   pl.BlockSpec((IDX_HEADS, tm), lambda i: (0, i))],
        out_shape=[jax.ShapeDtypeStruct((IDX_HEADS, IDX_HEAD_DIM, t), BF16),
                   jax.ShapeDtypeStruct((t, IDX_HEAD_DIM), BF16),
                   jax.ShapeDtypeStruct((IDX_HEADS, t), F32)],
        scratch_shapes=[pltpu.VMEM((tm, d), BF16)],
        compiler_params=_cparams(("parallel",)),
        name="idx",
    )(x, mod, gain, pos, ropec, w)


def _dsa_kernel(qt_ref, k_ref, vt_ref, qit_ref, ki_ref, wt_ref, og_ref, o_ref,
                key_ref, hi_ref, lo_ref, bias_ref, sa_ref, sb_ref, m_ref, l_ref, acc_ref, *, topk):
    _, tk, tq = key_ref.shape
    jq = pl.program_id(1)
    nt = jq + 1
    heads = qt_ref.shape[0] // GROUP_DIM
    q_chunk = (jq * tq + lax.broadcasted_iota(jnp.int32, (1, tq), 1)) // CHUNK

    def score_tile(kt, carry):
        k0 = pl.multiple_of(kt * tk, tk)
        for r0 in range(0, tk, SCORE_ROWS):
            ki = ki_ref[pl.ds(k0 + r0, SCORE_ROWS), :]
            acc = jnp.zeros((SCORE_ROWS, tq), F32)
            for hd in range(IDX_HEADS):
                logit = jnp.dot(ki, qit_ref[hd], preferred_element_type=F32)
                acc = acc + jnp.maximum(logit, 0.0) * wt_ref[hd:hd + 1, :]
            k_chunk = (k0 + r0 + lax.broadcasted_iota(jnp.int32, (SCORE_ROWS, 1), 0)) // CHUNK
            bits = pltpu.bitcast(jnp.where(k_chunk <= q_chunk, acc, NEG), jnp.int32)
            key = jnp.where(bits < 0, bits ^ 0x7FFFFFFF, bits)
            key_ref[kt, r0:r0 + SCORE_ROWS, :] = key
            hi_ref[kt, r0:r0 + SCORE_ROWS, :] = (key >> 16).astype(jnp.int16)
            lo_ref[kt, r0:r0 + SCORE_ROWS, :] = ((key & 0xFFFF) - 2 ** 15).astype(jnp.int16)
        return carry

    lax.fori_loop(0, nt, score_tile, 0)

    one16 = jnp.ones((I16_ROWS, tq), jnp.int16)
    zero16 = jnp.zeros((I16_ROWS, tq), jnp.int16)

    def count16(parts):
        return jnp.sum(sum(p.astype(jnp.int32) for p in parts), axis=0, keepdims=True)

    def select16(src_ref, need):
        def bit_step(i, u):
            cand_u = u | (jnp.int32(1) << (15 - i))
            cand = jnp.broadcast_to((cand_u - 2 ** 15).astype(jnp.int16), (I16_ROWS, tq))

            def count_tile(kt, parts):
                parts = list(parts)
                for r in range(tk // I16_ROWS):
                    ge = src_ref[kt, r * I16_ROWS:(r + 1) * I16_ROWS, :] >= cand
                    parts[r % COUNT_CHAINS] = parts[r % COUNT_CHAINS] + jnp.where(ge, one16, zero16)
                return tuple(parts)

            parts = lax.fori_loop(0, nt, count_tile, (zero16,) * COUNT_CHAINS)
            return jnp.where(count16(parts) >= need, cand_u, u)

        return lax.fori_loop(0, 16, bit_step, jnp.zeros((1, tq), jnp.int32))

    u_hi = select16(hi_ref, topk)
    t_hi = jnp.broadcast_to((u_hi - 2 ** 15).astype(jnp.int16), (I16_ROWS, tq))

    def split_tile(kt, parts):
        parts = list(parts)
        for r in range(tk // I16_ROWS):
            rows = slice(r * I16_ROWS, (r + 1) * I16_ROWS)
            hi = hi_ref[kt, rows, :]
            parts[r % COUNT_CHAINS] = parts[r % COUNT_CHAINS] + jnp.where(hi > t_hi, one16, zero16)
            lo_ref[kt, rows, :] = jnp.where(hi == t_hi, lo_ref[kt, rows, :], jnp.full_like(zero16, -(2 ** 15)))
        return tuple(parts)

    above = count16(lax.fori_loop(0, nt, split_tile, (zero16,) * COUNT_CHAINS))
    u_lo = select16(lo_ref, topk - above)
    thr = (u_hi - 2 ** 15) * 2 ** 16 + u_lo

    def bias_tile(kt, carry):
        bias_ref[kt] = jnp.where(key_ref[kt] >= thr, 0.0, NEG)
        return carry

    lax.fori_loop(0, nt, bias_tile, 0)
    k_chunk = (jq * tk + lax.broadcasted_iota(jnp.int32, (tk, tq), 0)) // CHUNK
    bias_ref[jq] = jnp.where(k_chunk <= q_chunk, bias_ref[jq], NEG)

    m_ref[...] = jnp.full(m_ref.shape, NEG, F32)
    l_ref[...] = jnp.zeros(l_ref.shape, F32)
    acc_ref[...] = jnp.zeros(acc_ref.shape, F32)

    def scores_into(kt, s_ref):
        k0 = pl.multiple_of(kt * tk, tk)
        for hd in range(heads):
            lo = hd * GROUP_DIM
            s_ref[hd] = jnp.dot(k_ref[pl.ds(k0, tk), lo:lo + GROUP_DIM], qt_ref[lo:lo + GROUP_DIM, :],
                                preferred_element_type=F32)

    def softmax_pv(kt, s_ref):
        for hd in range(heads):
            lo = hd * GROUP_DIM
            s = s_ref[hd] + bias_ref[kt]
            m_prev = m_ref[hd]
            m_new = jnp.maximum(m_prev, jnp.max(s, axis=0, keepdims=True))
            alpha = jnp.exp2(m_prev - m_new)
            p = jnp.exp2(s - m_new)
            m_ref[hd] = m_new
            l_ref[hd] = alpha * l_ref[hd] + jnp.sum(p, axis=0, keepdims=True)
            pv = jnp.dot(vt_ref[kt, lo:lo + GROUP_DIM, :], p.astype(BF16), preferred_element_type=F32)
            acc_ref[hd] = alpha * acc_ref[hd] + pv

    @pl.when(nt % 2 == 1)
    def _():
        bias_ref[nt] = jnp.full((tk, tq), NEG, F32)

    scores_into(0, sa_ref)

    def attn_pair(j, carry):
        t0 = 2 * j
        scores_into(t0 + 1, sb_ref)
        softmax_pv(t0, sa_ref)
        scores_into(jnp.minimum(t0 + 2, nt - 1), sa_ref)
        softmax_pv(t0 + 1, sb_ref)
        return carry

    lax.fori_loop(0, (nt + 1) // 2, attn_pair, 0)

    for hd in range(heads):
        lo = hd * GROUP_DIM
        o = jnp.transpose(acc_ref[hd] / l_ref[hd])
        ms = jnp.mean(o * o, axis=-1, keepdims=True)
        o = o * lax.rsqrt(ms + EPS) * og_ref[:, lo:lo + GROUP_DIM]
        o_ref[:, lo:lo + GROUP_DIM] = o.astype(o_ref.dtype)


def _dsa(qt, k, vt, qit, ki, wt, out_gain, *, bsz, seq):
    bw, t = qt.shape
    heads = bw // GROUP_DIM
    tq = tk = DSA_TILE
    nq = seq // tq
    topk = min(TOPK_MAX, seq // 4)
    assert tq >= topk and tq % CHUNK == 0 and nq % 2 == 0
    once = pl.Buffered(1)
    return pl.pallas_call(
        functools.partial(_dsa_kernel, topk=topk),
        grid=(bsz, nq),
        in_specs=[pl.BlockSpec((bw, tq), lambda b, j: (0, b * nq + j)),
                  pl.BlockSpec((seq, bw), lambda b, j: (b, 0), pipeline_mode=once),
                  pl.BlockSpec((nq, bw, tk), lambda b, j: (b, 0, 0), pipeline_mode=once),
                  pl.BlockSpec((IDX_HEADS, IDX_HEAD_DIM, tq), lambda b, j: (0, 0, b * nq + j)),
                  pl.BlockSpec((seq, IDX_HEAD_DIM), lambda b, j: (b, 0), pipeline_mode=once),
                  pl.BlockSpec((IDX_HEADS, tq), lambda b, j: (0, b * nq + j)),
                  pl.BlockSpec((1, bw), lambda b, j: (0, 0))],
        out_specs=pl.BlockSpec((tq, bw), lambda b, j: (b * nq + j, 0)),
        out_shape=jax.ShapeDtypeStruct((t, bw), BF16),
        scratch_shapes=[pltpu.VMEM((nq, tk, tq), jnp.int32),
                        pltpu.VMEM((nq, tk, tq), jnp.int16),
                        pltpu.VMEM((nq, tk, tq), jnp.int16),
                        pltpu.VMEM((nq, tk, tq), F32),
                        pltpu.VMEM((heads, tk, tq), F32),
                        pltpu.VMEM((heads, tk, tq), F32),
                        pltpu.VMEM((heads, 1, tq), F32),
                        pltpu.VMEM((heads, 1, tq), F32),
                        pltpu.VMEM((heads, GROUP_DIM, tq), F32)],
        compiler_params=_cparams(("parallel", "arbitrary")),
        name="dsa",
    )(qt, k, vt, qit, ki, wt, out_gain)


def _out_kernel(x_ref, mod_ref, a_ref, b_ref, wa_ref, wb_ref, o_ref, *, row0):
    mixed = jnp.dot(a_ref[...], wa_ref[...], preferred_element_type=F32)
    mixed = mixed + jnp.dot(b_ref[...], wb_ref[...], preferred_element_type=F32)
    o_ref[...] = x_ref[...] + mod_ref[0, row0:row0 + 1, :] * mixed


def _out(x, mod, ga, gb, w_out, *, row0, seq, tm=512, tn=1024):
    t, d = x.shape
    half = ga.shape[1]
    bpb = seq // tm
    return pl.pallas_call(
        functools.partial(_out_kernel, row0=row0),
        grid=(t // tm, d // tn),
        in_specs=[pl.BlockSpec((tm, tn), lambda i, n: (i, n)),
                  pl.BlockSpec((1, mod.shape[1], tn), lambda i, n: (i // bpb, 0, n)),
                  pl.BlockSpec((tm, half), lambda i, n: (i, 0)),
                  pl.BlockSpec((tm, half), lambda i, n: (i, 0)),
                  pl.BlockSpec((half, tn), lambda i, n: (0, n)),
                  pl.BlockSpec((half, tn), lambda i, n: (1, n))],
        out_specs=pl.BlockSpec((tm, tn), lambda i, n: (i, n)),
        out_shape=jax.ShapeDtypeStruct((t, d), F32),
        compiler_params=_cparams(("parallel", "arbitrary")),
        name="out",
    )(x, mod, ga, gb, w_out, w_out)


def _rope_consts(rot_dim, period):
    half = rot_dim // 2
    inv = ROPE_THETA ** (-2.0 * jnp.arange(half, dtype=F32) / rot_dim)
    lane = jnp.arange(LANES) % period
    inv_l = jnp.where(lane < rot_dim, inv[lane % half], 0.0)
    sa = jnp.where((lane >= half) & (lane < rot_dim), 1.0, 0.0)
    sb = jnp.where(lane < half, -1.0, 0.0)
    return jnp.zeros((8, LANES), F32).at[0].set(inv_l).at[1].set(sa).at[2].set(sb)


def kernel(x, c, positions, w_ada, b_ada, norm1_g, ffn1_w1, ffn1_w3, ffn1_w2, norm2_g, w_in, gmlp_v_g, gmlp_ws,
           gmlp_b, q_norm_g, k_norm_g, out_norm_g, w_out, norm3_g, ffn2_w1, ffn2_w3, ffn2_w2):
    bsz, seq, d = x.shape
    depth = w_ada.shape[0]
    aw = gmlp_v_g.shape[1]
    bw = d - aw
    t = bsz * seq

    xf = x.reshape(t, d)
    pos = positions.reshape(t, 1)
    c_pad = jnp.zeros((8, d), F32).at[:bsz].set(c)
    rope_b = _rope_consts(B_ROT_DIM, GROUP_DIM)
    rope_i = _rope_consts(IDX_ROT_DIM, IDX_HEAD_DIM)
    idx_cols = IDX_HEADS * IDX_HEAD_DIM + IDX_HEAD_DIM + IDX_HEADS
    idx_pad = -idx_cols % 256

    for l in range(depth):
        mod = _ada(c_pad, w_ada[l], b_ada[l][None, :])[:bsz].reshape(bsz, 9, d)

        xf = _ffn(xf, mod, norm1_g[l][None, :], ffn1_w1[l].astype(BF16), ffn1_w3[l].astype(BF16),
                  ffn1_w2[l].astype(BF16), row0=0, seq=seq)

        w_l = w_in[l]
        wu = w_l[:, :aw].astype(BF16)
        wv = w_l[:, aw:2 * aw].astype(BF16)
        wqkv = w_l[:, 2 * aw:2 * aw + 3 * bw].astype(BF16)
        widx = jnp.pad(w_l[:, 2 * aw + 3 * bw:], ((0, 0), (0, idx_pad))).astype(BF16)
        g2 = norm2_g[l][None, :]
        bs = jnp.broadcast_to(gmlp_b[l][:, :, None], gmlp_b.shape[1:] + (GROUP_DIM,))

        ga = _gmlp(xf, mod, g2, wu, wv, gmlp_v_g[l][None, :], gmlp_ws[l], bs, out_norm_g[l][None, :aw],
                   row0=3, seq=seq)
        qt, k, vt = _qkv(xf, mod, g2, pos, rope_b, wqkv, jnp.stack([q_norm_g[l], k_norm_g[l]]), row0=3, seq=seq)
        qit, ki, wt = _idx(xf, mod, g2, pos, rope_i, widx, row0=3, seq=seq)
        gb = _dsa(qt, k, vt, qit, ki, wt, out_norm_g[l][None, aw:], bsz=bsz, seq=seq)
        xf = _out(xf, mod, ga, gb, w_out[l].astype(BF16), row0=5, seq=seq)

        xf = _ffn(xf, mod, norm3_g[l][None, :], ffn2_w1[l].astype(BF16), ffn2_w3[l].astype(BF16),
                  ffn2_w2[l].astype(BF16), row0=6, seq=seq)
    return xf.reshape(bsz, seq, d)
```

```python
import functools

import jax
import jax.numpy as jnp
from jax import lax
from jax.experimental import pallas as pl
from jax.experimental.pallas import tpu as pltpu

F32 = jnp.float32
BF16 = jnp.bfloat16

CHUNK = 64
GMLP_WIN = 128
GROUP_DIM = 128
IDX_HEADS = 16
IDX_HEAD_DIM = 64
TOPK_MAX = 256
ROPE_THETA = 500000.0
B_ROT_DIM = GROUP_DIM // 4
IDX_ROT_DIM = IDX_HEAD_DIM // 4
EPS = 1e-6
NEG = -1e30

LANES = 128
I16_ROWS = 16
DENOM_ROWS = 16
LOG2E = 1.4426950408889634
NORM_ROWS = 16
NORM_UNROLL = 8
FFN_SUB = 256
PROJ_TN = 256
DSA_TILE = 256
SCORE_ROWS = 128
COUNT_CHAINS = 4
VMEM_LIMIT = 56 * 1024 * 1024


def _cparams(sem):
    return pltpu.CompilerParams(dimension_semantics=sem, vmem_limit_bytes=VMEM_LIMIT)


def _rms_mod_rows(x_ref, h_ref, g_ref, mod_ref, row0):
    gain = g_ref[...] * (1.0 + mod_ref[0, row0 + 1:row0 + 2, :])
    shift = mod_ref[0, row0:row0 + 1, :]

    def body(c, carry):
        r0 = pl.multiple_of(c * NORM_ROWS, NORM_ROWS)
        x = x_ref[pl.ds(r0, NORM_ROWS), :]
        ms = jnp.mean(x * x, axis=-1, keepdims=True)
        h_ref[pl.ds(r0, NORM_ROWS), :] = (x * lax.rsqrt(ms + EPS) * gain + shift).astype(h_ref.dtype)
        return carry

    lax.fori_loop(0, x_ref.shape[0] // NORM_ROWS, body, 0, unroll=NORM_UNROLL)


def _rope_tables(pos_ref, ropec_ref):
    ang = pos_ref[...].astype(F32) * ropec_ref[0:1, :]
    cos = jnp.cos(ang)
    sin = jnp.sin(ang)
    return cos, sin * ropec_ref[1:2, :], sin * ropec_ref[2:3, :]


def _rope_apply(x, cos, sa, sb, half):
    return x * cos + pltpu.roll(x, half, 1) * sa + pltpu.roll(x, LANES - half, 1) * sb


def _ada_kernel(c_ref, w_ref, b_ref, o_ref):
    c = c_ref[...]
    act = c * jax.nn.sigmoid(c)
    o_ref[...] = jnp.dot(act, w_ref[...], preferred_element_type=F32) + b_ref[...]


def _ada(c_pad, w_ada, b_ada, tn=1024):
    rows, d = c_pad.shape
    n = w_ada.shape[1]
    return pl.pallas_call(
        _ada_kernel,
        grid=(n // tn,),
        in_specs=[pl.BlockSpec((rows, d), lambda j: (0, 0)),
                  pl.BlockSpec((d, tn), lambda j: (0, j)),
                  pl.BlockSpec((1, tn), lambda j: (0, j))],
        out_specs=pl.BlockSpec((rows, tn), lambda j: (0, j)),
        out_shape=jax.ShapeDtypeStruct((rows, n), F32),
        compiler_params=_cparams(("arbitrary",)),
        name="ada",
    )(c_pad, w_ada, b_ada)


def _ffn_kernel(*refs, row0, mix_row):
    if mix_row is None:
        x_ref, mod_ref, g_ref, w1_ref, w3_ref, w2_ref, o_ref, h_ref = refs
        res_ref = x_ref
    else:
        x_ref, mod_ref, g_ref, w1_ref, w3_ref, w2_ref, a_ref, b_ref, wo_ref, o_ref, h_ref, res_ref = refs
    j = pl.program_id(1)

    @pl.when(j == 0)
    def _():
        if mix_row is not None:
            half = a_ref.shape[1]
            mixed = jnp.dot(a_ref[...], wo_ref[0:half, :], preferred_element_type=F32)
            mixed = mixed + jnp.dot(b_ref[...], wo_ref[half:, :], preferred_element_type=F32)
            res_ref[...] = x_ref[...] + mod_ref[0, mix_row:mix_row + 1, :] * mixed
        _rms_mod_rows(res_ref, h_ref, g_ref, mod_ref, row0)
        o_ref[...] = jnp.zeros_like(o_ref)

    h = h_ref[...]
    tf = w1_ref.shape[1]
    ab = [(jnp.dot(h, w1_ref[:, c0:c0 + FFN_SUB], preferred_element_type=F32),
           jnp.dot(h, w3_ref[:, c0:c0 + FFN_SUB], preferred_element_type=F32)) for c0 in range(0, tf, FFN_SUB)]
    upd = None
    for idx, (a, b) in enumerate(ab):
        act = (a * jax.nn.sigmoid(a) * b).astype(BF16)
        part = jnp.dot(act, w2_ref[idx * FFN_SUB:(idx + 1) * FFN_SUB, :], preferred_element_type=F32)
        upd = part if upd is None else upd + part
    o_ref[...] += upd

    @pl.when(j == pl.num_programs(1) - 1)
    def _():
        o_ref[...] = res_ref[...] + (0.5 * mod_ref[0, row0 + 2:row0 + 3, :]) * o_ref[...]


def _ffn(x, mod, gain, w1, w3, w2, *, row0, seq, mix=None, mix_row=None, tm=512, tf=512):
    t, d = x.shape
    f = w1.shape[1]
    bpb = seq // tm
    in_specs = [pl.BlockSpec((tm, d), lambda i, j: (i, 0)),
                pl.BlockSpec((1,) + mod.shape[1:], lambda i, j: (i // bpb, 0, 0)),
                pl.BlockSpec((1, d), lambda i, j: (0, 0)),
                pl.BlockSpec((d, tf), lambda i, j: (0, j)),
                pl.BlockSpec((d, tf), lambda i, j: (0, j)),
                pl.BlockSpec((tf, d), lambda i, j: (j, 0))]
    scratch = [pltpu.VMEM((tm, d), BF16)]
    args = (x, mod, gain, w1, w3, w2)
    if mix is not None:
        a, b, w_out = mix
        in_specs += [pl.BlockSpec((tm, a.shape[1]), lambda i, j: (i, 0)),
                     pl.BlockSpec((tm, b.shape[1]), lambda i, j: (i, 0)),
                     pl.BlockSpec(w_out.shape, lambda i, j: (0, 0), pipeline_mode=pl.Buffered(1))]
        scratch.append(pltpu.VMEM((tm, d), F32))
        args += (a, b, w_out)
    return pl.pallas_call(
        functools.partial(_ffn_kernel, row0=row0, mix_row=mix_row if mix is not None else None),
        grid=(t // tm, f // tf),
        in_specs=in_specs,
        out_specs=pl.BlockSpec((tm, d), lambda i, j: (i, 0)),
        out_shape=jax.ShapeDtypeStruct((t, d), F32),
        scratch_shapes=scratch,
        compiler_params=_cparams(("parallel", "arbitrary")),
        name="ffn",
    )(*args)


def _gmlp_kernel(x_ref, mod_ref, g_ref, wu_ref, wv_ref, vg_ref, ws_ref, bs_ref, og_ref, o_ref, h_ref, *, row0):
    _rms_mod_rows(x_ref, h_ref, g_ref, mod_ref, row0)
    tm = h_ref.shape[0]
    ri = lax.broadcasted_iota(jnp.int32, (GMLP_WIN, GMLP_WIN), 0) // CHUNK
    ci = lax.broadcasted_iota(jnp.int32, (GMLP_WIN, GMLP_WIN), 1) // CHUNK
    for c0 in range(0, wu_ref.shape[1], PROJ_TN):
        u = jax.nn.gelu(jnp.dot(h_ref[...], wu_ref[:, c0:c0 + PROJ_TN], preferred_element_type=F32))
        v = jax.nn.gelu(jnp.dot(h_ref[...], wv_ref[:, c0:c0 + PROJ_TN], preferred_element_type=F32))
        for sub in range(PROJ_TN // GROUP_DIM):
            lo = c0 + sub * GROUP_DIM
            g = lo // GROUP_DIM
            vg = v[:, sub * GROUP_DIM:(sub + 1) * GROUP_DIM]
            mu = jnp.mean(vg, axis=-1, keepdims=True)
            vc = vg - mu
            var = jnp.mean(vc * vc, axis=-1, keepdims=True)
            y = (vc * lax.rsqrt(var + EPS) * vg_ref[:, lo:lo + GROUP_DIM]).astype(BF16)
            w = jnp.where(ci <= ri, ws_ref[g], 0.0).astype(BF16)
            for win in range(tm // GMLP_WIN):
                r0 = win * GMLP_WIN
                mixed = jnp.dot(w, y[r0:r0 + GMLP_WIN], preferred_element_type=F32) + bs_ref[g]
                o = u[r0:r0 + GMLP_WIN, sub * GROUP_DIM:(sub + 1) * GROUP_DIM] * mixed
                ms = jnp.mean(o * o, axis=-1, keepdims=True)
                o = o * lax.rsqrt(ms + EPS) * og_ref[:, lo:lo + GROUP_DIM]
                o_ref[r0:r0 + GMLP_WIN, lo:lo + GROUP_DIM] = o.astype(o_ref.dtype)


def _gmlp(x, mod, gain, wu, wv, v_gain, ws, bs, out_gain, *, row0, seq, tm=512):
    t, d = x.shape
    aw = wu.shape[1]
    bpb = seq // tm
    once = pl.Buffered(1)
    return pl.pallas_call(
        functools.partial(_gmlp_kernel, row0=row0),
        grid=(t // tm,),
        in_specs=[pl.BlockSpec((tm, d), lambda i: (i, 0)),
                  pl.BlockSpec((1,) + mod.shape[1:], lambda i: (i // bpb, 0, 0)),
                  pl.BlockSpec((1, d), lambda i: (0, 0)),
                  pl.BlockSpec(wu.shape, lambda i: (0, 0), pipeline_mode=once),
                  pl.BlockSpec(wv.shape, lambda i: (0, 0), pipeline_mode=once),
                  pl.BlockSpec((1, aw), lambda i: (0, 0)),
                  pl.BlockSpec(ws.shape, lambda i: (0, 0, 0), pipeline_mode=once),
                  pl.BlockSpec(bs.shape, lambda i: (0, 0, 0), pipeline_mode=once),
                  pl.BlockSpec((1, aw), lambda i: (0, 0))],
        out_specs=pl.BlockSpec((tm, aw), lambda i: (i, 0)),
        out_shape=jax.ShapeDtypeStruct((t, aw), BF16),
        scratch_shapes=[pltpu.VMEM((tm, d), BF16)],
        compiler_params=_cparams(("parallel",)),
        name="gmlp",
    )(x, mod, gain, wu, wv, v_gain, ws, bs, out_gain)


def _qkv_kernel(x_ref, mod_ref, g_ref, pos_ref, ropec_ref, w_ref, qkg_ref, qt_ref, k_ref, vt_ref, h_ref,
                *, row0, attn_scale):
    _rms_mod_rows(x_ref, h_ref, g_ref, mod_ref, row0)
    cos, sa, sb = _rope_tables(pos_ref, ropec_ref)
    bw = k_ref.shape[1]

    def normed(xh, gain):
        ms = jnp.mean(xh * xh, axis=-1, keepdims=True)
        return _rope_apply(xh * lax.rsqrt(ms + EPS) * gain, cos, sa, sb, B_ROT_DIM // 2)

    for c0 in range(0, w_ref.shape[1], PROJ_TN):
        acc = jnp.dot(h_ref[...], w_ref[:, c0:c0 + PROJ_TN], preferred_element_type=F32)
        for sub in range(PROJ_TN // GROUP_DIM):
            xh = acc[:, sub * GROUP_DIM:(sub + 1) * GROUP_DIM]
            col = c0 + sub * GROUP_DIM
            if col < bw:
                qh = normed(xh, qkg_ref[0:1, :]) * attn_scale
                qt_ref[col:col + GROUP_DIM, :] = jnp.transpose(qh).astype(qt_ref.dtype)
            elif col < 2 * bw:
                k_ref[:, col - bw:col - bw + GROUP_DIM] = normed(xh, qkg_ref[1:2, :]).astype(k_ref.dtype)
            else:
                vh = jnp.transpose(xh).astype(vt_ref.dtype)
                for kb in range(vt_ref.shape[0]):
                    vt_ref[kb, col - 2 * bw:col - 2 * bw + GROUP_DIM, :] = vh[:, kb * DSA_TILE:(kb + 1) * DSA_TILE]


def _qkv(x, mod, gain, pos, ropec, w, qk_gain, *, row0, seq, tm=512):
    t, d = x.shape
    bw = w.shape[1] // 3
    bpb = seq // tm
    kern = functools.partial(_qkv_kernel, row0=row0, attn_scale=float(GROUP_DIM) ** -0.5 * LOG2E)
    return pl.pallas_call(
        kern,
        grid=(t // tm,),
        in_specs=[pl.BlockSpec((tm, d), lambda i: (i, 0)),
                  pl.BlockSpec((1,) + mod.shape[1:], lambda i: (i // bpb, 0, 0)),
                  pl.BlockSpec((1, d), lambda i: (0, 0)),
                  pl.BlockSpec((tm, 1), lambda i: (i, 0)),
                  pl.BlockSpec(ropec.shape, lambda i: (0, 0)),
                  pl.BlockSpec(w.shape, lambda i: (0, 0), pipeline_mode=pl.Buffered(1)),
                  pl.BlockSpec(qk_gain.shape, lambda i: (0, 0))],
        out_specs=[pl.BlockSpec((bw, tm), lambda i: (0, i)),
                   pl.BlockSpec((tm, bw), lambda i: (i, 0)),
                   pl.BlockSpec((tm // DSA_TILE, bw, DSA_TILE), lambda i: (i, 0, 0))],
        out_shape=[jax.ShapeDtypeStruct((bw, t), BF16),
                   jax.ShapeDtypeStruct((t, bw), BF16),
                   jax.ShapeDtypeStruct((t // DSA_TILE, bw, DSA_TILE), BF16)],
        scratch_shapes=[pltpu.VMEM((tm, d), BF16)],
        compiler_params=_cparams(("parallel",)),
        name="qkv",
    )(x, mod, gain, pos, ropec, w, qk_gain)


def _idx_kernel(x_ref, mod_ref, g_ref, pos_ref, ropec_ref, w_ref, qit_ref, ki_ref, wt_ref,
                h_ref, *, row0, w_scale):
    _rms_mod_rows(x_ref, h_ref, g_ref, mod_ref, row0)
    cos, sa, sb = _rope_tables(pos_ref, ropec_ref)
    half = IDX_ROT_DIM // 2
    q_cols = IDX_HEADS * IDX_HEAD_DIM

    for c0 in range(0, w_ref.shape[1], PROJ_TN):
        acc = jnp.dot(h_ref[...], w_ref[:, c0:c0 + PROJ_TN], preferred_element_type=F32)
        if c0 < q_cols:
            for c in range(PROJ_TN // LANES):
                xt = jnp.transpose(_rope_apply(acc[:, c * LANES:(c + 1) * LANES], cos, sa, sb, half))
                for sub in range(LANES // IDX_HEAD_DIM):
                    hd = (c0 + c * LANES) // IDX_HEAD_DIM + sub
                    qit_ref[hd] = xt[sub * IDX_HEAD_DIM:(sub + 1) * IDX_HEAD_DIM, :].astype(qit_ref.dtype)
        else:
            x = acc[:, 0:LANES]
            ki_ref[...] = _rope_apply(x, cos, sa, sb, half)[:, 0:IDX_HEAD_DIM].astype(ki_ref.dtype)
            wt_ref[...] = jnp.transpose(x * w_scale)[IDX_HEAD_DIM:IDX_HEAD_DIM + IDX_HEADS, :]


def _idx(x, mod, gain, pos, ropec, w, *, row0, seq, tm=512):
    t, d = x.shape
    bpb = seq // tm
    kern = functools.partial(_idx_kernel, row0=row0,
                             w_scale=float(IDX_HEADS) ** -0.5 * float(IDX_HEAD_DIM) ** -0.5)
    return pl.pallas_call(
        kern,
        grid=(t // tm,),
        in_specs=[pl.BlockSpec((tm, d), lambda i: (i, 0)),
                  pl.BlockSpec((1,) + mod.shape[1:], lambda i: (i // bpb, 0, 0)),
                  pl.BlockSpec((1, d), lambda i: (0, 0)),
                  pl.BlockSpec((tm, 1), lambda i: (i, 0)),
                  pl.BlockSpec(ropec.shape, lambda i: (0, 0)),
                  pl.BlockSpec(w.shape, lambda i: (0, 0), pipeline_mode=pl.Buffered(1))],
        out_specs=[pl.BlockSpec((IDX_HEADS, IDX_HEAD_DIM, tm), lambda i: (0, 0, i)),
                   pl.BlockSpec((tm, IDX_HEAD_DIM), lambda i: (i, 0)),
                   pl.BlockSpec((IDX_HEADS, tm), lambda i: (0, i))],
        out_shape=[jax.ShapeDtypeStruct((IDX_HEADS, IDX_HEAD_DIM, t), BF16),
                   jax.ShapeDtypeStruct((t, IDX_HEAD_DIM), BF16),
                   jax.ShapeDtypeStruct((IDX_HEADS, t), F32)],
        scratch_shapes=[pltpu.VMEM((tm, d), BF16)],
        compiler_params=_cparams(("parallel",)),
        name="idx",
    )(x, mod, gain, pos, ropec, w)


def _dsa_kernel(qt_ref, k_ref, vt_ref, qit_ref, ki_ref, wt_ref, og_ref, o_ref,
                key_ref, hi_ref, lo_ref, bias_ref, sa_ref, sb_ref, m_ref, acc_ref, *, topk):
    _, tk, tq = key_ref.shape
    jq = pl.program_id(1)
    nt = jq + 1
    heads = qt_ref.shape[0] // GROUP_DIM
    q_chunk = (jq * tq + lax.broadcasted_iota(jnp.int32, (1, tq), 1)) // CHUNK

    def score_tile(kt, carry):
        k0 = pl.multiple_of(kt * tk, tk)
        for r0 in range(0, tk, SCORE_ROWS):
            ki = ki_ref[pl.ds(k0 + r0, SCORE_ROWS), :]
            acc = jnp.zeros((SCORE_ROWS, tq), F32)
            for hd in range(IDX_HEADS):
                logit = jnp.dot(ki, qit_ref[hd], preferred_element_type=F32)
                acc = acc + jnp.maximum(logit, 0.0) * wt_ref[hd:hd + 1, :]
            k_chunk = (k0 + r0 + lax.broadcasted_iota(jnp.int32, (SCORE_ROWS, 1), 0)) // CHUNK
            bits = pltpu.bitcast(jnp.where(k_chunk <= q_chunk, acc, NEG), jnp.int32)
            key = jnp.where(bits < 0, bits ^ 0x7FFFFFFF, bits)
            key_ref[kt, r0:r0 + SCORE_ROWS, :] = key
            hi_ref[kt, r0:r0 + SCORE_ROWS, :] = (key >> 16).astype(jnp.int16)
            lo_ref[kt, r0:r0 + SCORE_ROWS, :] = ((key & 0xFFFF) - 2 ** 15).astype(jnp.int16)
        return carry

    lax.fori_loop(0, nt, score_tile, 0)

    one16 = jnp.ones((I16_ROWS, tq), jnp.int16)
    zero16 = jnp.zeros((I16_ROWS, tq), jnp.int16)

    def count16(parts):
        return jnp.sum(sum(p.astype(jnp.int32) for p in parts), axis=0, keepdims=True)

    def select16(src_ref, need):
        def bit_step(i, u):
            cand_u = u | (jnp.int32(1) << (15 - i))
            cand = jnp.broadcast_to((cand_u - 2 ** 15).astype(jnp.int16), (I16_ROWS, tq))

            def count_tile(kt, parts):
                parts = list(parts)
                for r in range(tk // I16_ROWS):
                    ge = src_ref[kt, r * I16_ROWS:(r + 1) * I16_ROWS, :] >= cand
                    parts[r % COUNT_CHAINS] = parts[r % COUNT_CHAINS] + jnp.where(ge, one16, zero16)
                return tuple(parts)

            parts = lax.fori_loop(0, nt, count_tile, (zero16,) * COUNT_CHAINS)
            return jnp.where(count16(parts) >= need, cand_u, u)

        return lax.fori_loop(0, 16, bit_step, jnp.zeros((1, tq), jnp.int32))

    u_hi = select16(hi_ref, topk)
    t_hi = jnp.broadcast_to((u_hi - 2 ** 15).astype(jnp.int16), (I16_ROWS, tq))

    def split_tile(kt, parts):
        parts = list(parts)
        for r in range(tk // I16_ROWS):
            rows = slice(r * I16_ROWS, (r + 1) * I16_ROWS)
            hi = hi_ref[kt, rows, :]
            parts[r % COUNT_CHAINS] = parts[r % COUNT_CHAINS] + jnp.where(hi > t_hi, one16, zero16)
            lo_ref[kt, rows, :] = jnp.where(hi == t_hi, lo_ref[kt, rows, :], jnp.full_like(zero16, -(2 ** 15)))
        return tuple(parts)

    above = count16(lax.fori_loop(0, nt, split_tile, (zero16,) * COUNT_CHAINS))
    u_lo = select16(lo_ref, topk - above)
    thr = (u_hi - 2 ** 15) * 2 ** 16 + u_lo

    def bias_tile(kt, carry):
        bias_ref[kt] = jnp.where(key_ref[kt] >= thr, 0.0, NEG)
        return carry

    lax.fori_loop(0, nt, bias_tile, 0)
    k_chunk = (jq * tk + lax.broadcasted_iota(jnp.int32, (tk, tq), 0)) // CHUNK
    bias_ref[jq] = jnp.where(k_chunk <= q_chunk, bias_ref[jq], NEG)

    m_ref[...] = jnp.full(m_ref.shape, NEG, F32)
    acc_ref[...] = jnp.zeros(acc_ref.shape, F32)
    ones_rows = jnp.ones((DENOM_ROWS, tk), BF16)

    def scores_into(kt, s_ref):
        k0 = pl.multiple_of(kt * tk, tk)
        for hd in range(heads):
            lo = hd * GROUP_DIM
            s_ref[hd] = jnp.dot(k_ref[pl.ds(k0, tk), lo:lo + GROUP_DIM], qt_ref[lo:lo + GROUP_DIM, :],
                                preferred_element_type=F32)

    def softmax_pv(kt, s_ref):
        for hd in range(heads):
            lo = hd * GROUP_DIM
            s = s_ref[hd] + bias_ref[kt]
            m_prev = m_ref[hd]
            m_new = jnp.maximum(m_prev, jnp.max(s, axis=0, keepdims=True))
            alpha = jnp.exp2(m_prev - m_new)
            p = jnp.exp2(s - m_new)
            m_ref[hd] = m_new
            vt1 = jnp.concatenate([vt_ref[kt, lo:lo + GROUP_DIM, :], ones_rows], axis=0)
            acc_ref[hd] = alpha * acc_ref[hd] + jnp.dot(vt1, p.astype(BF16), preferred_element_type=F32)

    @pl.when(nt % 2 == 1)
    def _():
        bias_ref[nt] = jnp.full((tk, tq), NEG, F32)

    scores_into(0, sa_ref)

    def attn_pair(j, carry):
        t0 = 2 * j
        scores_into(t0 + 1, sb_ref)
        softmax_pv(t0, sa_ref)
        scores_into(jnp.minimum(t0 + 2, nt - 1), sa_ref)
        softmax_pv(t0 + 1, sb_ref)
        return carry

    lax.fori_loop(0, (nt + 1) // 2, attn_pair, 0)

    for hd in range(heads):
        lo = hd * GROUP_DIM
        o = jnp.transpose(acc_ref[hd, 0:GROUP_DIM, :] / acc_ref[hd, GROUP_DIM:GROUP_DIM + 1, :])
        ms = jnp.mean(o * o, axis=-1, keepdims=True)
        o = o * lax.rsqrt(ms + EPS) * og_ref[:, lo:lo + GROUP_DIM]
        o_ref[:, lo:lo + GROUP_DIM] = o.astype(o_ref.dtype)


def _dsa(qt, k, vt, qit, ki, wt, out_gain, *, bsz, seq):
    bw, t = qt.shape
    heads = bw // GROUP_DIM
    tq = tk = DSA_TILE
    nq = seq // tq
    topk = min(TOPK_MAX, seq // 4)
    assert tq >= topk and tq % CHUNK == 0 and nq % 2 == 0
    once = pl.Buffered(1)
    return pl.pallas_call(
        functools.partial(_dsa_kernel, topk=topk),
        grid=(bsz, nq),
        in_specs=[pl.BlockSpec((bw, tq), lambda b, j: (0, b * nq + j)),
                  pl.BlockSpec((seq, bw), lambda b, j: (b, 0), pipeline_mode=once),
                  pl.BlockSpec((nq, bw, tk), lambda b, j: (b, 0, 0), pipeline_mode=once),
                  pl.BlockSpec((IDX_HEADS, IDX_HEAD_DIM, tq), lambda b, j: (0, 0, b * nq + j)),
                  pl.BlockSpec((seq, IDX_HEAD_DIM), lambda b, j: (b, 0), pipeline_mode=once),
                  pl.BlockSpec((IDX_HEADS, tq), lambda b, j: (0, b * nq + j)),
                  pl.BlockSpec((1, bw), lambda b, j: (0, 0))],
        out_specs=pl.BlockSpec((tq, bw), lambda b, j: (b * nq + j, 0)),
        out_shape=jax.ShapeDtypeStruct((t, bw), BF16),
        scratch_shapes=[pltpu.VMEM((nq, tk, tq), jnp.int32),
                        pltpu.VMEM((nq, tk, tq), jnp.int16),
                        pltpu.VMEM((nq, tk, tq), jnp.int16),
                        pltpu.VMEM((nq, tk, tq), F32),
                        pltpu.VMEM((heads, tk, tq), F32),
                        pltpu.VMEM((heads, tk, tq), F32),
                        pltpu.VMEM((heads, 1, tq), F32),
                        pltpu.VMEM((heads, GROUP_DIM + DENOM_ROWS, tq), F32)],
        compiler_params=_cparams(("parallel", "arbitrary")),
        name="dsa",
    )(qt, k, vt, qit, ki, wt, out_gain)


def _rope_consts(rot_dim, period):
    half = rot_dim // 2
    inv = ROPE_THETA ** (-2.0 * jnp.arange(half, dtype=F32) / rot_dim)
    lane = jnp.arange(LANES) % period
    inv_l = jnp.where(lane < rot_dim, inv[lane % half], 0.0)
    sa = jnp.where((lane >= half) & (lane < rot_dim), 1.0, 0.0)
    sb = jnp.where(lane < half, -1.0, 0.0)
    return jnp.zeros((8, LANES), F32).at[0].set(inv_l).at[1].set(sa).at[2].set(sb)


def kernel(x, c, positions, w_ada, b_ada, norm1_g, ffn1_w1, ffn1_w3, ffn1_w2, norm2_g, w_in, gmlp_v_g, gmlp_ws,
           gmlp_b, q_norm_g, k_norm_g, out_norm_g, w_out, norm3_g, ffn2_w1, ffn2_w3, ffn2_w2):
    bsz, seq, d = x.shape
    depth = w_ada.shape[0]
    aw = gmlp_v_g.shape[1]
    bw = d - aw
    t = bsz * seq

    xf = x.reshape(t, d)
    pos = positions.reshape(t, 1)
    c_pad = jnp.zeros((8, d), F32).at[:bsz].set(c)
    rope_b = _rope_consts(B_ROT_DIM, GROUP_DIM)
    rope_i = _rope_consts(IDX_ROT_DIM, IDX_HEAD_DIM)
    idx_cols = IDX_HEADS * IDX_HEAD_DIM + IDX_HEAD_DIM + IDX_HEADS
    idx_pad = -idx_cols % 256

    for l in range(depth):
        mod = _ada(c_pad, w_ada[l], b_ada[l][None, :])[:bsz].reshape(bsz, 9, d)

        xf = _ffn(xf, mod, norm1_g[l][None, :], ffn1_w1[l].astype(BF16), ffn1_w3[l].astype(BF16),
                  ffn1_w2[l].astype(BF16), row0=0, seq=seq)

        w_l = w_in[l]
        wu = w_l[:, :aw].astype(BF16)
        wv = w_l[:, aw:2 * aw].astype(BF16)
        wqkv = w_l[:, 2 * aw:2 * aw + 3 * bw].astype(BF16)
        widx = jnp.pad(w_l[:, 2 * aw + 3 * bw:], ((0, 0), (0, idx_pad))).astype(BF16)
        g2 = norm2_g[l][None, :]
        bs = jnp.broadcast_to(gmlp_b[l][:, :, None], gmlp_b.shape[1:] + (GROUP_DIM,))

        ga = _gmlp(xf, mod, g2, wu, wv, gmlp_v_g[l][None, :], gmlp_ws[l], bs, out_norm_g[l][None, :aw],
                   row0=3, seq=seq)
        qt, k, vt = _qkv(xf, mod, g2, pos, rope_b, wqkv, jnp.stack([q_norm_g[l], k_norm_g[l]]), row0=3, seq=seq)
        qit, ki, wt = _idx(xf, mod, g2, pos, rope_i, widx, row0=3, seq=seq)
        gb = _dsa(qt, k, vt, qit, ki, wt, out_norm_g[l][None, aw:], bsz=bsz, seq=seq)
        xf = _ffn(xf, mod, norm3_g[l][None, :], ffn2_w1[l].astype(BF16), ffn2_w3[l].astype(BF16),
                  ffn2_w2[l].astype(BF16), row0=6, seq=seq, mix=(ga, gb, w_out[l].astype(BF16)), mix_row=5)
    return xf.reshape(bsz, seq, d)
```

```python
import functools

import jax
import jax.numpy as jnp
from jax import lax
from jax.experimental import pallas as pl
from jax.experimental.pallas import tpu as pltpu

F32 = jnp.float32
BF16 = jnp.bfloat16

CHUNK = 64
GMLP_WIN = 128
GROUP_DIM = 128
IDX_HEADS = 16
IDX_HEAD_DIM = 64
TOPK_MAX = 256
ROPE_THETA = 500000.0
B_ROT_DIM = GROUP_DIM // 4
IDX_ROT_DIM = IDX_HEAD_DIM // 4
EPS = 1e-6
NEG = -1e30

LANES = 128
I16_ROWS = 16
DENOM_ROWS = 16
LOG2E = 1.4426950408889634
NORM_ROWS = 16
NORM_UNROLL = 8
FFN_SUB = 256
PROJ_TN = 256
DSA_TILE = 256
SCORE_ROWS = 128
COUNT_CHAINS = 4
VMEM_LIMIT = 62 * 1024 * 1024


def _cparams(sem):
    return pltpu.CompilerParams(dimension_semantics=sem, vmem_limit_bytes=VMEM_LIMIT)


def _rms_mod_rows(x_ref, h_ref, g_ref, mod_ref, row0):
    gain = g_ref[...] * (1.0 + mod_ref[0, row0 + 1:row0 + 2, :])
    shift = mod_ref[0, row0:row0 + 1, :]

    def body(c, carry):
        r0 = pl.multiple_of(c * NORM_ROWS, NORM_ROWS)
        x = x_ref[pl.ds(r0, NORM_ROWS), :]
        ms = jnp.mean(x * x, axis=-1, keepdims=True)
        h_ref[pl.ds(r0, NORM_ROWS), :] = (x * lax.rsqrt(ms + EPS) * gain + shift).astype(h_ref.dtype)
        return carry

    lax.fori_loop(0, x_ref.shape[0] // NORM_ROWS, body, 0, unroll=NORM_UNROLL)


def _rope_tables(pos_ref, ropec_ref):
    ang = pos_ref[...].astype(F32) * ropec_ref[0:1, :]
    cos = jnp.cos(ang)
    sin = jnp.sin(ang)
    return cos, sin * ropec_ref[1:2, :], sin * ropec_ref[2:3, :]


def _rope_apply(x, cos, sa, sb, half):
    return x * cos + pltpu.roll(x, half, 1) * sa + pltpu.roll(x, LANES - half, 1) * sb


def _ada_kernel(c_ref, w_ref, b_ref, o_ref):
    c = c_ref[...]
    act = c * jax.nn.sigmoid(c)
    o_ref[...] = jnp.dot(act, w_ref[...], preferred_element_type=F32) + b_ref[...]


def _ada(c_pad, w_ada, b_ada, tn=1024):
    rows, d = c_pad.shape
    n = w_ada.shape[1]
    return pl.pallas_call(
        _ada_kernel,
        grid=(n // tn,),
        in_specs=[pl.BlockSpec((rows, d), lambda j: (0, 0)),
                  pl.BlockSpec((d, tn), lambda j: (0, j)),
                  pl.BlockSpec((1, tn), lambda j: (0, j))],
        out_specs=pl.BlockSpec((rows, tn), lambda j: (0, j)),
        out_shape=jax.ShapeDtypeStruct((rows, n), F32),
        compiler_params=_cparams(("arbitrary",)),
        name="ada",
    )(c_pad, w_ada, b_ada)


def _ffn_kernel(*refs, row0, mix_row):
    if mix_row is None:
        x_ref, mod_ref, g_ref, w1_ref, w3_ref, w2_ref, o_ref, h_ref = refs
        res_ref = x_ref
    else:
        x_ref, mod_ref, g_ref, w1_ref, w3_ref, w2_ref, a_ref, b_ref, wo_ref, o_ref, h_ref, res_ref = refs
    j = pl.program_id(1)

    @pl.when(j == 0)
    def _():
        if mix_row is not None:
            half = a_ref.shape[1]
            mixed = jnp.dot(a_ref[...], wo_ref[0:half, :], preferred_element_type=F32)
            mixed = mixed + jnp.dot(b_ref[...], wo_ref[half:, :], preferred_element_type=F32)
            res_ref[...] = x_ref[...] + mod_ref[0, mix_row:mix_row + 1, :] * mixed
        _rms_mod_rows(res_ref, h_ref, g_ref, mod_ref, row0)
        o_ref[...] = jnp.zeros_like(o_ref)

    h = h_ref[...]
    tf = w1_ref.shape[1]
    ab = [(jnp.dot(h, w1_ref[:, c0:c0 + FFN_SUB], preferred_element_type=F32),
           jnp.dot(h, w3_ref[:, c0:c0 + FFN_SUB], preferred_element_type=F32)) for c0 in range(0, tf, FFN_SUB)]
    upd = None
    for idx, (a, b) in enumerate(ab):
        act = (a * jax.nn.sigmoid(a) * b).astype(BF16)
        part = jnp.dot(act, w2_ref[idx * FFN_SUB:(idx + 1) * FFN_SUB, :], preferred_element_type=F32)
        upd = part if upd is None else upd + part
    o_ref[...] += upd

    @pl.when(j == pl.num_programs(1) - 1)
    def _():
        o_ref[...] = res_ref[...] + (0.5 * mod_ref[0, row0 + 2:row0 + 3, :]) * o_ref[...]


def _ffn(x, mod, gain, w1, w3, w2, *, row0, seq, mix=None, mix_row=None, tm=512, tf=512):
    t, d = x.shape
    f = w1.shape[1]
    bpb = seq // tm
    in_specs = [pl.BlockSpec((tm, d), lambda i, j: (i, 0)),
                pl.BlockSpec((1,) + mod.shape[1:], lambda i, j: (i // bpb, 0, 0)),
                pl.BlockSpec((1, d), lambda i, j: (0, 0)),
                pl.BlockSpec((d, tf), lambda i, j: (0, j)),
                pl.BlockSpec((d, tf), lambda i, j: (0, j)),
                pl.BlockSpec((tf, d), lambda i, j: (j, 0))]
    scratch = [pltpu.VMEM((tm, d), BF16)]
    args = (x, mod, gain, w1, w3, w2)
    if mix is not None:
        a, b, w_out = mix
        in_specs += [pl.BlockSpec((tm, a.shape[1]), lambda i, j: (i, 0)),
                     pl.BlockSpec((tm, b.shape[1]), lambda i, j: (i, 0)),
                     pl.BlockSpec(w_out.shape, lambda i, j: (0, 0), pipeline_mode=pl.Buffered(1))]
        scratch.append(pltpu.VMEM((tm, d), F32))
        args += (a, b, w_out)
    return pl.pallas_call(
        functools.partial(_ffn_kernel, row0=row0, mix_row=mix_row if mix is not None else None),
        grid=(t // tm, f // tf),
        in_specs=in_specs,
        out_specs=pl.BlockSpec((tm, d), lambda i, j: (i, 0)),
        out_shape=jax.ShapeDtypeStruct((t, d), F32),
        scratch_shapes=scratch,
        compiler_params=_cparams(("parallel", "arbitrary")),
        name="ffn",
    )(*args)


def _gmlp_cols(h_ref, wu_ref, wv_ref, vg_ref, ws_ref, bs_ref, og_ref, o_ref):
    tm = h_ref.shape[0]
    ri = lax.broadcasted_iota(jnp.int32, (GMLP_WIN, GMLP_WIN), 0) // CHUNK
    ci = lax.broadcasted_iota(jnp.int32, (GMLP_WIN, GMLP_WIN), 1) // CHUNK
    for c0 in range(0, wu_ref.shape[1], PROJ_TN):
        u = jax.nn.gelu(jnp.dot(h_ref[...], wu_ref[:, c0:c0 + PROJ_TN], preferred_element_type=F32))
        v = jax.nn.gelu(jnp.dot(h_ref[...], wv_ref[:, c0:c0 + PROJ_TN], preferred_element_type=F32))
        for sub in range(PROJ_TN // GROUP_DIM):
            lo = c0 + sub * GROUP_DIM
            g = lo // GROUP_DIM
            vg = v[:, sub * GROUP_DIM:(sub + 1) * GROUP_DIM]
            mu = jnp.mean(vg, axis=-1, keepdims=True)
            vc = vg - mu
            var = jnp.mean(vc * vc, axis=-1, keepdims=True)
            y = (vc * lax.rsqrt(var + EPS) * vg_ref[:, lo:lo + GROUP_DIM]).astype(BF16)
            w = jnp.where(ci <= ri, ws_ref[g], 0.0).astype(BF16)
            for win in range(tm // GMLP_WIN):
                r0 = win * GMLP_WIN
                mixed = jnp.dot(w, y[r0:r0 + GMLP_WIN], preferred_element_type=F32) + bs_ref[g]
                o = u[r0:r0 + GMLP_WIN, sub * GROUP_DIM:(sub + 1) * GROUP_DIM] * mixed
                ms = jnp.mean(o * o, axis=-1, keepdims=True)
                o = o * lax.rsqrt(ms + EPS) * og_ref[:, lo:lo + GROUP_DIM]
                o_ref[r0:r0 + GMLP_WIN, lo:lo + GROUP_DIM] = o.astype(o_ref.dtype)


def _qkv_cols(h_ref, pos_ref, ropec_ref, w_ref, qkg_ref, qt_ref, k_ref, vt_ref, attn_scale):
    cos, sa, sb = _rope_tables(pos_ref, ropec_ref)
    bw = k_ref.shape[1]

    def normed(xh, gain):
        ms = jnp.mean(xh * xh, axis=-1, keepdims=True)
        return _rope_apply(xh * lax.rsqrt(ms + EPS) * gain, cos, sa, sb, B_ROT_DIM // 2)

    for c0 in range(0, w_ref.shape[1], PROJ_TN):
        acc = jnp.dot(h_ref[...], w_ref[:, c0:c0 + PROJ_TN], preferred_element_type=F32)
        for sub in range(PROJ_TN // GROUP_DIM):
            xh = acc[:, sub * GROUP_DIM:(sub + 1) * GROUP_DIM]
            col = c0 + sub * GROUP_DIM
            if col < bw:
                qh = normed(xh, qkg_ref[0:1, :]) * attn_scale
                qt_ref[col:col + GROUP_DIM, :] = jnp.transpose(qh).astype(qt_ref.dtype)
            elif col < 2 * bw:
                k_ref[:, col - bw:col - bw + GROUP_DIM] = normed(xh, qkg_ref[1:2, :]).astype(k_ref.dtype)
            else:
                vh = jnp.transpose(xh).astype(vt_ref.dtype)
                for kb in range(vt_ref.shape[0]):
                    vt_ref[kb, col - 2 * bw:col - 2 * bw + GROUP_DIM, :] = vh[:, kb * DSA_TILE:(kb + 1) * DSA_TILE]


def _idx_cols(h_ref, pos_ref, ropec_ref, w_ref, qit_ref, ki_ref, wt_ref, w_scale):
    cos, sa, sb = _rope_tables(pos_ref, ropec_ref)
    half = IDX_ROT_DIM // 2
    q_cols = IDX_HEADS * IDX_HEAD_DIM

    for c0 in range(0, w_ref.shape[1], PROJ_TN):
        acc = jnp.dot(h_ref[...], w_ref[:, c0:c0 + PROJ_TN], preferred_element_type=F32)
        if c0 < q_cols:
            for c in range(PROJ_TN // LANES):
                xt = jnp.transpose(_rope_apply(acc[:, c * LANES:(c + 1) * LANES], cos, sa, sb, half))
                for sub in range(LANES // IDX_HEAD_DIM):
                    hd = (c0 + c * LANES) // IDX_HEAD_DIM + sub
                    qit_ref[hd] = xt[sub * IDX_HEAD_DIM:(sub + 1) * IDX_HEAD_DIM, :].astype(qit_ref.dtype)
        else:
            x = acc[:, 0:LANES]
            ki_ref[...] = _rope_apply(x, cos, sa, sb, half)[:, 0:IDX_HEAD_DIM].astype(ki_ref.dtype)
            wt_ref[...] = jnp.transpose(x * w_scale)[IDX_HEAD_DIM:IDX_HEAD_DIM + IDX_HEADS, :]


def _proj_kernel(x_ref, mod_ref, g_ref, pos_ref, ropeb_ref, ropei_ref,
                 wu_ref, wv_ref, vg_ref, ws_ref, bs_ref, og_ref, wqkv_ref, qkg_ref, widx_ref,
                 ga_ref, qt_ref, k_ref, vt_ref, qit_ref, ki_ref, wt_ref, h_ref, *, row0, attn_scale, w_scale):
    x = x_ref[...]
    ms = jnp.mean(x * x, axis=-1, keepdims=True)
    h = x * lax.rsqrt(ms + EPS) * g_ref[...]
    h_ref[...] = (h * (1.0 + mod_ref[0, row0 + 1:row0 + 2, :]) + mod_ref[0, row0:row0 + 1, :]).astype(BF16)
    _idx_cols(h_ref, pos_ref, ropei_ref, widx_ref, qit_ref, ki_ref, wt_ref, w_scale)
    _qkv_cols(h_ref, pos_ref, ropeb_ref, wqkv_ref, qkg_ref, qt_ref, k_ref, vt_ref, attn_scale)
    _gmlp_cols(h_ref, wu_ref, wv_ref, vg_ref, ws_ref, bs_ref, og_ref, ga_ref)


def _proj(x, mod, gain, pos, rope_b, rope_i, wu, wv, v_gain, ws, bs, out_gain, wqkv, qk_gain, widx,
          *, row0, seq, tm=512):
    t, d = x.shape
    aw = wu.shape[1]
    bw = wqkv.shape[1] // 3
    bpb = seq // tm
    once = pl.Buffered(1)

    def resident(a):
        return pl.BlockSpec(a.shape, lambda i: (0,) * a.ndim, pipeline_mode=once)

    kern = functools.partial(_proj_kernel, row0=row0, attn_scale=float(GROUP_DIM) ** -0.5 * LOG2E,
                             w_scale=float(IDX_HEADS) ** -0.5 * float(IDX_HEAD_DIM) ** -0.5)
    return pl.pallas_call(
        kern,
        grid=(t // tm,),
        in_specs=[pl.BlockSpec((tm, d), lambda i: (i, 0)),
                  pl.BlockSpec((1,) + mod.shape[1:], lambda i: (i // bpb, 0, 0)),
                  pl.BlockSpec((1, d), lambda i: (0, 0)),
                  pl.BlockSpec((tm, 1), lambda i: (i, 0)),
                  resident(rope_b), resident(rope_i),
                  resident(wu), resident(wv), resident(v_gain), resident(ws), resident(bs), resident(out_gain),
                  resident(wqkv), resident(qk_gain), resident(widx)],
        out_specs=[pl.BlockSpec((tm, aw), lambda i: (i, 0)),
                   pl.BlockSpec((bw, tm), lambda i: (0, i)),
                   pl.BlockSpec((tm, bw), lambda i: (i, 0)),
                   pl.BlockSpec((tm // DSA_TILE, bw, DSA_TILE), lambda i: (i, 0, 0)),
                   pl.BlockSpec((IDX_HEADS, IDX_HEAD_DIM, tm), lambda i: (0, 0, i)),
                   pl.BlockSpec((tm, IDX_HEAD_DIM), lambda i: (i, 0)),
                   pl.BlockSpec((IDX_HEADS, tm), lambda i: (0, i))],
        out_shape=[jax.ShapeDtypeStruct((t, aw), BF16),
                   jax.ShapeDtypeStruct((bw, t), BF16),
                   jax.ShapeDtypeStruct((t, bw), BF16),
                   jax.ShapeDtypeStruct((t // DSA_TILE, bw, DSA_TILE), BF16),
                   jax.ShapeDtypeStruct((IDX_HEADS, IDX_HEAD_DIM, t), BF16),
                   jax.ShapeDtypeStruct((t, IDX_HEAD_DIM), BF16),
                   jax.ShapeDtypeStruct((IDX_HEADS, t), F32)],
        scratch_shapes=[pltpu.VMEM((tm, d), BF16)],
        compiler_params=_cparams(("parallel",)),
        name="proj",
    )(x, mod, gain, pos, rope_b, rope_i, wu, wv, v_gain, ws, bs, out_gain, wqkv, qk_gain, widx)


def _dsa_kernel(qt_ref, k_ref, vt_ref, qit_ref, ki_ref, wt_ref, og_ref, o_ref,
                key_ref, hi_ref, lo_ref, bias_ref, sa_ref, sb_ref, m_ref, acc_ref, *, topk):
    _, tk, tq = key_ref.shape
    jq = pl.program_id(1)
    nt = jq + 1
    heads = qt_ref.shape[0] // GROUP_DIM
    q_chunk = (jq * tq + lax.broadcasted_iota(jnp.int32, (1, tq), 1)) // CHUNK

    def score_tile(kt, carry):
        k0 = pl.multiple_of(kt * tk, tk)
        for r0 in range(0, tk, SCORE_ROWS):
            ki = ki_ref[pl.ds(k0 + r0, SCORE_ROWS), :]
            acc = jnp.zeros((SCORE_ROWS, tq), F32)
            for hd in range(IDX_HEADS):
                logit = jnp.dot(ki, qit_ref[hd], preferred_element_type=F32)
                acc = acc + jnp.maximum(logit, 0.0) * wt_ref[hd:hd + 1, :]
            k_chunk = (k0 + r0 + lax.broadcasted_iota(jnp.int32, (SCORE_ROWS, 1), 0)) // CHUNK
            bits = pltpu.bitcast(jnp.where(k_chunk <= q_chunk, acc, NEG), jnp.int32)
            key = jnp.where(bits < 0, bits ^ 0x7FFFFFFF, bits)
            key_ref[kt, r0:r0 + SCORE_ROWS, :] = key
            hi_ref[kt, r0:r0 + SCORE_ROWS, :] = (key >> 16).astype(jnp.int16)
            lo_ref[kt, r0:r0 + SCORE_ROWS, :] = ((key & 0xFFFF) - 2 ** 15).astype(jnp.int16)
        return carry

    lax.fori_loop(0, nt, score_tile, 0)

    one16 = jnp.ones((I16_ROWS, tq), jnp.int16)
    zero16 = jnp.zeros((I16_ROWS, tq), jnp.int16)

    def count16(parts):
        return jnp.sum(sum(p.astype(jnp.int32) for p in parts), axis=0, keepdims=True)

    def select16(src_ref, need):
        def bit_step(i, u):
            cand_u = u | (jnp.int32(1) << (15 - i))
            cand = jnp.broadcast_to((cand_u - 2 ** 15).astype(jnp.int16), (I16_ROWS, tq))

            def count_tile(kt, parts):
                parts = list(parts)
                for r in range(tk // I16_ROWS):
                    ge = src_ref[kt, r * I16_ROWS:(r + 1) * I16_ROWS, :] >= cand
                    parts[r % COUNT_CHAINS] = parts[r % COUNT_CHAINS] + jnp.where(ge, one16, zero16)
                return tuple(parts)

            parts = lax.fori_loop(0, nt, count_tile, (zero16,) * COUNT_CHAINS)
            return jnp.where(count16(parts) >= need, cand_u, u)

        return lax.fori_loop(0, 16, bit_step, jnp.zeros((1, tq), jnp.int32))

    u_hi = select16(hi_ref, topk)
    t_hi = jnp.broadcast_to((u_hi - 2 ** 15).astype(jnp.int16), (I16_ROWS, tq))

    def split_tile(kt, parts):
        parts = list(parts)
        for r in range(tk // I16_ROWS):
            rows = slice(r * I16_ROWS, (r + 1) * I16_ROWS)
            hi = hi_ref[kt, rows, :]
            parts[r % COUNT_CHAINS] = parts[r % COUNT_CHAINS] + jnp.where(hi > t_hi, one16, zero16)
            lo_ref[kt, rows, :] = jnp.where(hi == t_hi, lo_ref[kt, rows, :], jnp.full_like(zero16, -(2 ** 15)))
        return tuple(parts)

    above = count16(lax.fori_loop(0, nt, split_tile, (zero16,) * COUNT_CHAINS))
    u_lo = select16(lo_ref, topk - above)
    thr = (u_hi - 2 ** 15) * 2 ** 16 + u_lo

    def bias_tile(kt, carry):
        bias_ref[kt] = jnp.where(key_ref[kt] >= thr, 0.0, NEG)
        return carry

    lax.fori_loop(0, nt, bias_tile, 0)
    k_chunk = (jq * tk + lax.broadcasted_iota(jnp.int32, (tk, tq), 0)) // CHUNK
    bias_ref[jq] = jnp.where(k_chunk <= q_chunk, bias_ref[jq], NEG)

    m_ref[...] = jnp.full(m_ref.shape, NEG, F32)
    acc_ref[...] = jnp.zeros(acc_ref.shape, F32)
    ones_rows = jnp.ones((DENOM_ROWS, tk), BF16)

    def score_head(kt, s_ref, hd):
        k0 = pl.multiple_of(kt * tk, tk)
        lo = hd * GROUP_DIM
        s_ref[hd] = jnp.dot(k_ref[pl.ds(k0, tk), lo:lo + GROUP_DIM], qt_ref[lo:lo + GROUP_DIM, :],
                            preferred_element_type=F32)

    def softmax_pv_head(kt, s_ref, hd):
        lo = hd * GROUP_DIM
        ps, alphas = [], []
        for c0 in range(0, tq, LANES):
            s = s_ref[hd, :, c0:c0 + LANES] + bias_ref[kt, :, c0:c0 + LANES]
            m_prev = m_ref[hd, :, c0:c0 + LANES]
            m_new = jnp.maximum(m_prev, jnp.max(s, axis=0, keepdims=True))
            m_ref[hd, :, c0:c0 + LANES] = m_new
            alphas.append(jnp.exp2(m_prev - m_new))
            ps.append(jnp.exp2(s - m_new).astype(BF16))
        alpha = jnp.concatenate(alphas, axis=1)
        vt1 = jnp.concatenate([vt_ref[kt, lo:lo + GROUP_DIM, :], ones_rows], axis=0)
        acc_ref[hd] = alpha * acc_ref[hd] + jnp.dot(vt1, jnp.concatenate(ps, axis=1), preferred_element_type=F32)

    def scores_into(kt, s_ref):
        for hd in range(heads):
            score_head(kt, s_ref, hd)

    def softmax_pv(kt, s_ref, next_kt, next_ref):
        for hd in range(heads):
            score_head(next_kt, next_ref, hd)
            softmax_pv_head(kt, s_ref, hd)

    @pl.when(nt % 2 == 1)
    def _():
        bias_ref[nt] = jnp.full((tk, tq), NEG, F32)

    scores_into(0, sa_ref)

    def attn_pair(j, carry):
        t0 = 2 * j
        softmax_pv(t0, sa_ref, t0 + 1, sb_ref)
        softmax_pv(t0 + 1, sb_ref, jnp.minimum(t0 + 2, nt - 1), sa_ref)
        return carry

    lax.fori_loop(0, (nt + 1) // 2, attn_pair, 0)

    for hd in range(heads):
        lo = hd * GROUP_DIM
        o = jnp.transpose(acc_ref[hd, 0:GROUP_DIM, :] / acc_ref[hd, GROUP_DIM:GROUP_DIM + 1, :])
        ms = jnp.mean(o * o, axis=-1, keepdims=True)
        o = o * lax.rsqrt(ms + EPS) * og_ref[:, lo:lo + GROUP_DIM]
        o_ref[:, lo:lo + GROUP_DIM] = o.astype(o_ref.dtype)


def _dsa(qt, k, vt, qit, ki, wt, out_gain, *, bsz, seq):
    bw, t = qt.shape
    heads = bw // GROUP_DIM
    tq = tk = DSA_TILE
    nq = seq // tq
    topk = min(TOPK_MAX, seq // 4)
    assert tq >= topk and tq % CHUNK == 0 and nq % 2 == 0
    once = pl.Buffered(1)
    return pl.pallas_call(
        functools.partial(_dsa_kernel, topk=topk),
        grid=(bsz, nq),
        in_specs=[pl.BlockSpec((bw, tq), lambda b, j: (0, b * nq + j)),
                  pl.BlockSpec((seq, bw), lambda b, j: (b, 0), pipeline_mode=once),
                  pl.BlockSpec((nq, bw, tk), lambda b, j: (b, 0, 0), pipeline_mode=once),
                  pl.BlockSpec((IDX_HEADS, IDX_HEAD_DIM, tq), lambda b, j: (0, 0, b * nq + j)),
                  pl.BlockSpec((seq, IDX_HEAD_DIM), lambda b, j: (b, 0), pipeline_mode=once),
                  pl.BlockSpec((IDX_HEADS, tq), lambda b, j: (0, b * nq + j)),
                  pl.BlockSpec((1, bw), lambda b, j: (0, 0))],
        out_specs=pl.BlockSpec((tq, bw), lambda b, j: (b * nq + j, 0)),
        out_shape=jax.ShapeDtypeStruct((t, bw), BF16),
        scratch_shapes=[pltpu.VMEM((nq, tk, tq), jnp.int32),
                        pltpu.VMEM((nq, tk, tq), jnp.int16),
                        pltpu.VMEM((nq, tk, tq), jnp.int16),
                        pltpu.VMEM((nq, tk, tq), F32),
                        pltpu.VMEM((heads, tk, tq), F32),
                        pltpu.VMEM((heads, tk, tq), F32),
                        pltpu.VMEM((heads, 1, tq), F32),
                        pltpu.VMEM((heads, GROUP_DIM + DENOM_ROWS, tq), F32)],
        compiler_params=_cparams(("parallel", "arbitrary")),
        name="dsa",
    )(qt, k, vt, qit, ki, wt, out_gain)


def _rope_consts(rot_dim, period):
    half = rot_dim // 2
    inv = ROPE_THETA ** (-2.0 * jnp.arange(half, dtype=F32) / rot_dim)
    lane = jnp.arange(LANES) % period
    inv_l = jnp.where(lane < rot_dim, inv[lane % half], 0.0)
    sa = jnp.where((lane >= half) & (lane < rot_dim), 1.0, 0.0)
    sb = jnp.where(lane < half, -1.0, 0.0)
    return jnp.zeros((8, LANES), F32).at[0].set(inv_l).at[1].set(sa).at[2].set(sb)


def kernel(x, c, positions, w_ada, b_ada, norm1_g, ffn1_w1, ffn1_w3, ffn1_w2, norm2_g, w_in, gmlp_v_g, gmlp_ws,
           gmlp_b, q_norm_g, k_norm_g, out_norm_g, w_out, norm3_g, ffn2_w1, ffn2_w3, ffn2_w2):
    bsz, seq, d = x.shape
    depth = w_ada.shape[0]
    aw = gmlp_v_g.shape[1]
    bw = d - aw
    t = bsz * seq

    xf = x.reshape(t, d)
    pos = positions.reshape(t, 1)
    c_pad = jnp.zeros((8, d), F32).at[:bsz].set(c)
    rope_b = _rope_consts(B_ROT_DIM, GROUP_DIM)
    rope_i = _rope_consts(IDX_ROT_DIM, IDX_HEAD_DIM)
    idx_cols = IDX_HEADS * IDX_HEAD_DIM + IDX_HEAD_DIM + IDX_HEADS
    idx_pad = -idx_cols % 256

    for l in range(depth):
        mod = _ada(c_pad, w_ada[l], b_ada[l][None, :])[:bsz].reshape(bsz, 9, d)

        xf = _ffn(xf, mod, norm1_g[l][None, :], ffn1_w1[l].astype(BF16), ffn1_w3[l].astype(BF16),
                  ffn1_w2[l].astype(BF16), row0=0, seq=seq)

        w_l = w_in[l]
        wu = w_l[:, :aw].astype(BF16)
        wv = w_l[:, aw:2 * aw].astype(BF16)
        wqkv = w_l[:, 2 * aw:2 * aw + 3 * bw].astype(BF16)
        widx = jnp.pad(w_l[:, 2 * aw + 3 * bw:], ((0, 0), (0, idx_pad))).astype(BF16)
        g2 = norm2_g[l][None, :]
        bs = jnp.broadcast_to(gmlp_b[l][:, :, None], gmlp_b.shape[1:] + (GROUP_DIM,))

        ga, qt, k, vt, qit, ki, wt = _proj(
            xf, mod, g2, pos, rope_b, rope_i, wu, wv, gmlp_v_g[l][None, :], gmlp_ws[l], bs,
            out_norm_g[l][None, :aw], wqkv, jnp.stack([q_norm_g[l], k_norm_g[l]]), widx, row0=3, seq=seq)
        gb = _dsa(qt, k, vt, qit, ki, wt, out_norm_g[l][None, aw:], bsz=bsz, seq=seq)
        xf = _ffn(xf, mod, norm3_g[l][None, :], ffn2_w1[l].astype(BF16), ffn2_w3[l].astype(BF16),
                  ffn2_w2[l].astype(BF16), row0=6, seq=seq, mix=(ga, gb, w_out[l].astype(BF16)), mix_row=5)
    return xf.reshape(bsz, seq, d)
```

```python
import functools

import jax
import jax.numpy as jnp
from jax import lax
from jax.experimental import pallas as pl
from jax.experimental.pallas import tpu as pltpu

F32 = jnp.float32
BF16 = jnp.bfloat16

CHUNK = 64
GMLP_WIN = 128
GROUP_DIM = 128
IDX_HEADS = 16
IDX_HEAD_DIM = 64
TOPK_MAX = 256
ROPE_THETA = 500000.0
B_ROT_DIM = GROUP_DIM // 4
IDX_ROT_DIM = IDX_HEAD_DIM // 4
EPS = 1e-6
NEG = -1e30

LANES = 128
I16_ROWS = 16
DENOM_ROWS = 16
LOG2E = 1.4426950408889634
NORM_ROWS = 16
NORM_UNROLL = 8
FFN_SUB = 256
PROJ_TN = 256
DSA_TILE = 256
SCORE_ROWS = 128
COUNT_CHAINS = 4
VMEM_LIMIT = 62 * 1024 * 1024


def _cparams(sem):
    return pltpu.CompilerParams(dimension_semantics=sem, vmem_limit_bytes=VMEM_LIMIT)


def _rms_mod_rows(x_ref, h_ref, g_ref, mod_ref, row0):
    gain = g_ref[...] * (1.0 + mod_ref[0, row0 + 1:row0 + 2, :])
    shift = mod_ref[0, row0:row0 + 1, :]

    def body(c, carry):
        r0 = pl.multiple_of(c * NORM_ROWS, NORM_ROWS)
        x = x_ref[pl.ds(r0, NORM_ROWS), :]
        ms = jnp.mean(x * x, axis=-1, keepdims=True)
        h_ref[pl.ds(r0, NORM_ROWS), :] = (x * lax.rsqrt(ms + EPS) * gain + shift).astype(h_ref.dtype)
        return carry

    lax.fori_loop(0, x_ref.shape[0] // NORM_ROWS, body, 0, unroll=NORM_UNROLL)


def _rope_tables(pos_ref, ropec_ref):
    ang = pos_ref[...].astype(F32) * ropec_ref[0:1, :]
    cos = jnp.cos(ang)
    sin = jnp.sin(ang)
    return cos, sin * ropec_ref[1:2, :], sin * ropec_ref[2:3, :]


def _rope_apply(x, cos, sa, sb, half):
    return x * cos + pltpu.roll(x, half, 1) * sa + pltpu.roll(x, LANES - half, 1) * sb


def _ada_kernel(c_ref, w_ref, b_ref, o_ref):
    c = c_ref[...]
    act = c * jax.nn.sigmoid(c)
    o_ref[...] = jnp.dot(act, w_ref[...], preferred_element_type=F32) + b_ref[...]


def _ada(c_pad, w_ada, b_ada, tn=1024):
    rows, d = c_pad.shape
    n = w_ada.shape[1]
    return pl.pallas_call(
        _ada_kernel,
        grid=(n // tn,),
        in_specs=[pl.BlockSpec((rows, d), lambda j: (0, 0)),
                  pl.BlockSpec((d, tn), lambda j: (0, j)),
                  pl.BlockSpec((1, tn), lambda j: (0, j))],
        out_specs=pl.BlockSpec((rows, tn), lambda j: (0, j)),
        out_shape=jax.ShapeDtypeStruct((rows, n), F32),
        compiler_params=_cparams(("arbitrary",)),
        name="ada",
    )(c_pad, w_ada, b_ada)


def _ffn_kernel(x_ref, mod_ref, g_ref, w1_ref, w3_ref, w2_ref, o_ref, h_ref, *, row0):
    j = pl.program_id(1)

    @pl.when(j == 0)
    def _():
        _rms_mod_rows(x_ref, h_ref, g_ref, mod_ref, row0)
        o_ref[...] = jnp.zeros_like(o_ref)

    h = h_ref[...]
    tf = w1_ref.shape[1]
    ab = [(jnp.dot(h, w1_ref[:, c0:c0 + FFN_SUB], preferred_element_type=F32),
           jnp.dot(h, w3_ref[:, c0:c0 + FFN_SUB], preferred_element_type=F32)) for c0 in range(0, tf, FFN_SUB)]
    upd = None
    for idx, (a, b) in enumerate(ab):
        act = (a * jax.nn.sigmoid(a) * b).astype(BF16)
        part = jnp.dot(act, w2_ref[idx * FFN_SUB:(idx + 1) * FFN_SUB, :], preferred_element_type=F32)
        upd = part if upd is None else upd + part
    o_ref[...] += upd

    @pl.when(j == pl.num_programs(1) - 1)
    def _():
        o_ref[...] = x_ref[...] + (0.5 * mod_ref[0, row0 + 2:row0 + 3, :]) * o_ref[...]


def _ffn(x, mod, gain, w1, w3, w2, *, row0, seq, tm=1024, tf=512):
    t, d = x.shape
    f = w1.shape[1]
    bpb = seq // tm
    return pl.pallas_call(
        functools.partial(_ffn_kernel, row0=row0),
        grid=(t // tm, f // tf),
        in_specs=[pl.BlockSpec((tm, d), lambda i, j: (i, 0)),
                  pl.BlockSpec((1,) + mod.shape[1:], lambda i, j: (i // bpb, 0, 0)),
                  pl.BlockSpec((1, d), lambda i, j: (0, 0)),
                  pl.BlockSpec((d, tf), lambda i, j: (0, j)),
                  pl.BlockSpec((d, tf), lambda i, j: (0, j)),
                  pl.BlockSpec((tf, d), lambda i, j: (j, 0))],
        out_specs=pl.BlockSpec((tm, d), lambda i, j: (i, 0)),
        out_shape=jax.ShapeDtypeStruct((t, d), F32),
        scratch_shapes=[pltpu.VMEM((tm, d), BF16)],
        compiler_params=_cparams(("parallel", "arbitrary")),
        name="ffn",
    )(x, mod, gain, w1, w3, w2)


def _out_kernel(x_ref, mod_ref, a_ref, b_ref, wo_ref, o_ref, *, row0):
    half = a_ref.shape[1]
    mixed = jnp.dot(a_ref[...], wo_ref[0:half, :], preferred_element_type=F32)
    mixed = mixed + jnp.dot(b_ref[...], wo_ref[half:, :], preferred_element_type=F32)
    o_ref[...] = x_ref[...] + mod_ref[0, row0:row0 + 1, :] * mixed


def _out(x, mod, ga, gb, w_out, *, row0, seq, tm=512):
    t, d = x.shape
    bpb = seq // tm
    return pl.pallas_call(
        functools.partial(_out_kernel, row0=row0),
        grid=(t // tm,),
        in_specs=[pl.BlockSpec((tm, d), lambda i: (i, 0)),
                  pl.BlockSpec((1,) + mod.shape[1:], lambda i: (i // bpb, 0, 0)),
                  pl.BlockSpec((tm, ga.shape[1]), lambda i: (i, 0)),
                  pl.BlockSpec((tm, gb.shape[1]), lambda i: (i, 0)),
                  pl.BlockSpec(w_out.shape, lambda i: (0, 0), pipeline_mode=pl.Buffered(1))],
        out_specs=pl.BlockSpec((tm, d), lambda i: (i, 0)),
        out_shape=jax.ShapeDtypeStruct((t, d), F32),
        compiler_params=_cparams(("parallel",)),
        name="out",
    )(x, mod, ga, gb, w_out)


def _gmlp_cols(h_ref, wu_ref, wv_ref, vg_ref, ws_ref, bs_ref, og_ref, o_ref):
    tm = h_ref.shape[0]
    ri = lax.broadcasted_iota(jnp.int32, (GMLP_WIN, GMLP_WIN), 0) // CHUNK
    ci = lax.broadcasted_iota(jnp.int32, (GMLP_WIN, GMLP_WIN), 1) // CHUNK
    for c0 in range(0, wu_ref.shape[1], PROJ_TN):
        u = jax.nn.gelu(jnp.dot(h_ref[...], wu_ref[:, c0:c0 + PROJ_TN], preferred_element_type=F32))
        v = jax.nn.gelu(jnp.dot(h_ref[...], wv_ref[:, c0:c0 + PROJ_TN], preferred_element_type=F32))
        for sub in range(PROJ_TN // GROUP_DIM):
            lo = c0 + sub * GROUP_DIM
            g = lo // GROUP_DIM
            vg = v[:, sub * GROUP_DIM:(sub + 1) * GROUP_DIM]
            mu = jnp.mean(vg, axis=-1, keepdims=True)
            vc = vg - mu
            var = jnp.mean(vc * vc, axis=-1, keepdims=True)
            y = (vc * lax.rsqrt(var + EPS) * vg_ref[:, lo:lo + GROUP_DIM]).astype(BF16)
            w = jnp.where(ci <= ri, ws_ref[g], 0.0).astype(BF16)
            for win in range(tm // GMLP_WIN):
                r0 = win * GMLP_WIN
                mixed = jnp.dot(w, y[r0:r0 + GMLP_WIN], preferred_element_type=F32) + bs_ref[g]
                o = u[r0:r0 + GMLP_WIN, sub * GROUP_DIM:(sub + 1) * GROUP_DIM] * mixed
                ms = jnp.mean(o * o, axis=-1, keepdims=True)
                o = o * lax.rsqrt(ms + EPS) * og_ref[:, lo:lo + GROUP_DIM]
                o_ref[r0:r0 + GMLP_WIN, lo:lo + GROUP_DIM] = o.astype(o_ref.dtype)


def _qkv_cols(h_ref, pos_ref, ropec_ref, w_ref, qkg_ref, qt_ref, k_ref, vt_ref, attn_scale):
    cos, sa, sb = _rope_tables(pos_ref, ropec_ref)
    bw = k_ref.shape[1]

    def normed(xh, gain):
        ms = jnp.mean(xh * xh, axis=-1, keepdims=True)
        return _rope_apply(xh * lax.rsqrt(ms + EPS) * gain, cos, sa, sb, B_ROT_DIM // 2)

    for c0 in range(0, w_ref.shape[1], PROJ_TN):
        acc = jnp.dot(h_ref[...], w_ref[:, c0:c0 + PROJ_TN], preferred_element_type=F32)
        for sub in range(PROJ_TN // GROUP_DIM):
            xh = acc[:, sub * GROUP_DIM:(sub + 1) * GROUP_DIM]
            col = c0 + sub * GROUP_DIM
            if col < bw:
                qh = normed(xh, qkg_ref[0:1, :]) * attn_scale
                qt_ref[col:col + GROUP_DIM, :] = jnp.transpose(qh).astype(qt_ref.dtype)
            elif col < 2 * bw:
                k_ref[:, col - bw:col - bw + GROUP_DIM] = normed(xh, qkg_ref[1:2, :]).astype(k_ref.dtype)
            else:
                vh = jnp.transpose(xh).astype(vt_ref.dtype)
                for kb in range(vt_ref.shape[0]):
                    vt_ref[kb, col - 2 * bw:col - 2 * bw + GROUP_DIM, :] = vh[:, kb * DSA_TILE:(kb + 1) * DSA_TILE]


def _idx_cols(h_ref, pos_ref, ropec_ref, w_ref, qit_ref, ki_ref, wt_ref, w_scale):
    cos, sa, sb = _rope_tables(pos_ref, ropec_ref)
    half = IDX_ROT_DIM // 2
    q_cols = IDX_HEADS * IDX_HEAD_DIM

    for c0 in range(0, w_ref.shape[1], PROJ_TN):
        acc = jnp.dot(h_ref[...], w_ref[:, c0:c0 + PROJ_TN], preferred_element_type=F32)
        if c0 < q_cols:
            for c in range(PROJ_TN // LANES):
                xt = jnp.transpose(_rope_apply(acc[:, c * LANES:(c + 1) * LANES], cos, sa, sb, half))
                for sub in range(LANES // IDX_HEAD_DIM):
                    hd = (c0 + c * LANES) // IDX_HEAD_DIM + sub
                    qit_ref[hd] = xt[sub * IDX_HEAD_DIM:(sub + 1) * IDX_HEAD_DIM, :].astype(qit_ref.dtype)
        else:
            x = acc[:, 0:LANES]
            ki_ref[...] = _rope_apply(x, cos, sa, sb, half)[:, 0:IDX_HEAD_DIM].astype(ki_ref.dtype)
            wt_ref[...] = jnp.transpose(x * w_scale)[IDX_HEAD_DIM:IDX_HEAD_DIM + IDX_HEADS, :]


def _proj_kernel(x_ref, mod_ref, g_ref, pos_ref, ropeb_ref, ropei_ref,
                 wu_ref, wv_ref, vg_ref, ws_ref, bs_ref, og_ref, wqkv_ref, qkg_ref, widx_ref,
                 ga_ref, qt_ref, k_ref, vt_ref, qit_ref, ki_ref, wt_ref, h_ref, *, row0, attn_scale, w_scale):
    x = x_ref[...]
    ms = jnp.mean(x * x, axis=-1, keepdims=True)
    h = x * lax.rsqrt(ms + EPS) * g_ref[...]
    h_ref[...] = (h * (1.0 + mod_ref[0, row0 + 1:row0 + 2, :]) + mod_ref[0, row0:row0 + 1, :]).astype(BF16)
    _idx_cols(h_ref, pos_ref, ropei_ref, widx_ref, qit_ref, ki_ref, wt_ref, w_scale)
    _qkv_cols(h_ref, pos_ref, ropeb_ref, wqkv_ref, qkg_ref, qt_ref, k_ref, vt_ref, attn_scale)
    _gmlp_cols(h_ref, wu_ref, wv_ref, vg_ref, ws_ref, bs_ref, og_ref, ga_ref)


def _proj(x, mod, gain, pos, rope_b, rope_i, wu, wv, v_gain, ws, bs, out_gain, wqkv, qk_gain, widx,
          *, row0, seq, tm=512):
    t, d = x.shape
    aw = wu.shape[1]
    bw = wqkv.shape[1] // 3
    bpb = seq // tm
    once = pl.Buffered(1)

    def resident(a):
        return pl.BlockSpec(a.shape, lambda i: (0,) * a.ndim, pipeline_mode=once)

    kern = functools.partial(_proj_kernel, row0=row0, attn_scale=float(GROUP_DIM) ** -0.5 * LOG2E,
                             w_scale=float(IDX_HEADS) ** -0.5 * float(IDX_HEAD_DIM) ** -0.5)
    return pl.pallas_call(
        kern,
        grid=(t // tm,),
        in_specs=[pl.BlockSpec((tm, d), lambda i: (i, 0)),
                  pl.BlockSpec((1,) + mod.shape[1:], lambda i: (i // bpb, 0, 0)),
                  pl.BlockSpec((1, d), lambda i: (0, 0)),
                  pl.BlockSpec((tm, 1), lambda i: (i, 0)),
                  resident(rope_b), resident(rope_i),
                  resident(wu), resident(wv), resident(v_gain), resident(ws), resident(bs), resident(out_gain),
                  resident(wqkv), resident(qk_gain), resident(widx)],
        out_specs=[pl.BlockSpec((tm, aw), lambda i: (i, 0)),
                   pl.BlockSpec((bw, tm), lambda i: (0, i)),
                   pl.BlockSpec((tm, bw), lambda i: (i, 0)),
                   pl.BlockSpec((tm // DSA_TILE, bw, DSA_TILE), lambda i: (i, 0, 0)),
                   pl.BlockSpec((IDX_HEADS, IDX_HEAD_DIM, tm), lambda i: (0, 0, i)),
                   pl.BlockSpec((tm, IDX_HEAD_DIM), lambda i: (i, 0)),
                   pl.BlockSpec((IDX_HEADS, tm), lambda i: (0, i))],
        out_shape=[jax.ShapeDtypeStruct((t, aw), BF16),
                   jax.ShapeDtypeStruct((bw, t), BF16),
                   jax.ShapeDtypeStruct((t, bw), BF16),
                   jax.ShapeDtypeStruct((t // DSA_TILE, bw, DSA_TILE), BF16),
                   jax.ShapeDtypeStruct((IDX_HEADS, IDX_HEAD_DIM, t), BF16),
                   jax.ShapeDtypeStruct((t, IDX_HEAD_DIM), BF16),
                   jax.ShapeDtypeStruct((IDX_HEADS, t), F32)],
        scratch_shapes=[pltpu.VMEM((tm, d), BF16)],
        compiler_params=_cparams(("parallel",)),
        name="proj",
    )(x, mod, gain, pos, rope_b, rope_i, wu, wv, v_gain, ws, bs, out_gain, wqkv, qk_gain, widx)


def _dsa_kernel(qt_ref, k_ref, vt_ref, qit_ref, ki_ref, wt_ref, og_ref, o_ref,
                key_ref, hi_ref, lo_ref, bias_ref, sa_ref, sb_ref, m_ref, acc_ref, *, topk):
    _, tk, tq = key_ref.shape
    jq = pl.program_id(1)
    nt = jq + 1
    heads = qt_ref.shape[0] // GROUP_DIM
    q_chunk = (jq * tq + lax.broadcasted_iota(jnp.int32, (1, tq), 1)) // CHUNK

    def score_tile(kt, carry):
        k0 = pl.multiple_of(kt * tk, tk)
        for r0 in range(0, tk, SCORE_ROWS):
            ki = ki_ref[pl.ds(k0 + r0, SCORE_ROWS), :]
            acc = jnp.zeros((SCORE_ROWS, tq), F32)
            for hd in range(IDX_HEADS):
                logit = jnp.dot(ki, qit_ref[hd], preferred_element_type=F32)
                acc = acc + jnp.maximum(logit, 0.0) * wt_ref[hd:hd + 1, :]
            k_chunk = (k0 + r0 + lax.broadcasted_iota(jnp.int32, (SCORE_ROWS, 1), 0)) // CHUNK
            bits = pltpu.bitcast(jnp.where(k_chunk <= q_chunk, acc, NEG), jnp.int32)
            key = jnp.where(bits < 0, bits ^ 0x7FFFFFFF, bits)
            key_ref[kt, r0:r0 + SCORE_ROWS, :] = key
            hi_ref[kt, r0:r0 + SCORE_ROWS, :] = (key >> 16).astype(jnp.int16)
            lo_ref[kt, r0:r0 + SCORE_ROWS, :] = ((key & 0xFFFF) - 2 ** 15).astype(jnp.int16)
        return carry

    lax.fori_loop(0, nt, score_tile, 0)

    one16 = jnp.ones((I16_ROWS, tq), jnp.int16)
    zero16 = jnp.zeros((I16_ROWS, tq), jnp.int16)

    def count16(parts):
        return jnp.sum(sum(p.astype(jnp.int32) for p in parts), axis=0, keepdims=True)

    def select16(src_ref, need):
        def bit_step(i, u):
            cand_u = u | (jnp.int32(1) << (15 - i))
            cand = jnp.broadcast_to((cand_u - 2 ** 15).astype(jnp.int16), (I16_ROWS, tq))

            def count_tile(kt, parts):
                parts = list(parts)
                for r in range(tk // I16_ROWS):
                    ge = src_ref[kt, r * I16_ROWS:(r + 1) * I16_ROWS, :] >= cand
                    parts[r % COUNT_CHAINS] = parts[r % COUNT_CHAINS] + jnp.where(ge, one16, zero16)
                return tuple(parts)

            parts = lax.fori_loop(0, nt, count_tile, (zero16,) * COUNT_CHAINS)
            return jnp.where(count16(parts) >= need, cand_u, u)

        return lax.fori_loop(0, 16, bit_step, jnp.zeros((1, tq), jnp.int32))

    u_hi = select16(hi_ref, topk)
    t_hi = jnp.broadcast_to((u_hi - 2 ** 15).astype(jnp.int16), (I16_ROWS, tq))

    def split_tile(kt, parts):
        parts = list(parts)
        for r in range(tk // I16_ROWS):
            rows = slice(r * I16_ROWS, (r + 1) * I16_ROWS)
            hi = hi_ref[kt, rows, :]
            parts[r % COUNT_CHAINS] = parts[r % COUNT_CHAINS] + jnp.where(hi > t_hi, one16, zero16)
            lo_ref[kt, rows, :] = jnp.where(hi == t_hi, lo_ref[kt, rows, :], jnp.full_like(zero16, -(2 ** 15)))
        return tuple(parts)

    above = count16(lax.fori_loop(0, nt, split_tile, (zero16,) * COUNT_CHAINS))
    u_lo = select16(lo_ref, topk - above)
    thr = (u_hi - 2 ** 15) * 2 ** 16 + u_lo

    def bias_tile(kt, carry):
        bias_ref[kt] = jnp.where(key_ref[kt] >= thr, 0.0, NEG)
        return carry

    lax.fori_loop(0, nt, bias_tile, 0)
    k_chunk = (jq * tk + lax.broadcasted_iota(jnp.int32, (tk, tq), 0)) // CHUNK
    bias_ref[jq] = jnp.where(k_chunk <= q_chunk, bias_ref[jq], NEG)

    m_ref[...] = jnp.full(m_ref.shape, NEG, F32)
    acc_ref[...] = jnp.zeros(acc_ref.shape, F32)
    ones_rows = jnp.ones((DENOM_ROWS, tk), BF16)

    def score_head(kt, s_ref, hd):
        k0 = pl.multiple_of(kt * tk, tk)
        lo = hd * GROUP_DIM
        s_ref[hd] = jnp.dot(k_ref[pl.ds(k0, tk), lo:lo + GROUP_DIM], qt_ref[lo:lo + GROUP_DIM, :],
                            preferred_element_type=F32)

    def softmax_pv_head(kt, s_ref, hd):
        lo = hd * GROUP_DIM
        ps, alphas = [], []
        for c0 in range(0, tq, LANES):
            s = s_ref[hd, :, c0:c0 + LANES] + bias_ref[kt, :, c0:c0 + LANES]
            m_prev = m_ref[hd, :, c0:c0 + LANES]
            m_new = jnp.maximum(m_prev, jnp.max(s, axis=0, keepdims=True))
            m_ref[hd, :, c0:c0 + LANES] = m_new
            alphas.append(jnp.exp2(m_prev - m_new))
            ps.append(jnp.exp2(s - m_new).astype(BF16))
        alpha = jnp.concatenate(alphas, axis=1)
        vt1 = jnp.concatenate([vt_ref[kt, lo:lo + GROUP_DIM, :], ones_rows], axis=0)
        acc_ref[hd] = alpha * acc_ref[hd] + jnp.dot(vt1, jnp.concatenate(ps, axis=1), preferred_element_type=F32)

    def scores_into(kt, s_ref):
        for hd in range(heads):
            score_head(kt, s_ref, hd)

    def softmax_pv(kt, s_ref, next_kt, next_ref):
        for hd in range(heads):
            score_head(next_kt, next_ref, hd)
            softmax_pv_head(kt, s_ref, hd)

    @pl.when(nt % 2 == 1)
    def _():
        bias_ref[nt] = jnp.full((tk, tq), NEG, F32)

    scores_into(0, sa_ref)

    def attn_pair(j, carry):
        t0 = 2 * j
        softmax_pv(t0, sa_ref, t0 + 1, sb_ref)
        softmax_pv(t0 + 1, sb_ref, jnp.minimum(t0 + 2, nt - 1), sa_ref)
        return carry

    lax.fori_loop(0, (nt + 1) // 2, attn_pair, 0)

    for hd in range(heads):
        lo = hd * GROUP_DIM
        o = jnp.transpose(acc_ref[hd, 0:GROUP_DIM, :] / acc_ref[hd, GROUP_DIM:GROUP_DIM + 1, :])
        ms = jnp.mean(o * o, axis=-1, keepdims=True)
        o = o * lax.rsqrt(ms + EPS) * og_ref[:, lo:lo + GROUP_DIM]
        o_ref[:, lo:lo + GROUP_DIM] = o.astype(o_ref.dtype)


def _dsa(qt, k, vt, qit, ki, wt, out_gain, *, bsz, seq):
    bw, t = qt.shape
    heads = bw // GROUP_DIM
    tq = tk = DSA_TILE
    nq = seq // tq
    topk = min(TOPK_MAX, seq // 4)
    assert tq >= topk and tq % CHUNK == 0 and nq % 2 == 0
    once = pl.Buffered(1)
    return pl.pallas_call(
        functools.partial(_dsa_kernel, topk=topk),
        grid=(bsz, nq),
        in_specs=[pl.BlockSpec((bw, tq), lambda b, j: (0, b * nq + j)),
                  pl.BlockSpec((seq, bw), lambda b, j: (b, 0), pipeline_mode=once),
                  pl.BlockSpec((nq, bw, tk), lambda b, j: (b, 0, 0), pipeline_mode=once),
                  pl.BlockSpec((IDX_HEADS, IDX_HEAD_DIM, tq), lambda b, j: (0, 0, b * nq + j)),
                  pl.BlockSpec((seq, IDX_HEAD_DIM), lambda b, j: (b, 0), pipeline_mode=once),
                  pl.BlockSpec((IDX_HEADS, tq), lambda b, j: (0, b * nq + j)),
                  pl.BlockSpec((1, bw), lambda b, j: (0, 0))],
        out_specs=pl.BlockSpec((tq, bw), lambda b, j: (b * nq + j, 0)),
        out_shape=jax.ShapeDtypeStruct((t, bw), BF16),
        scratch_shapes=[pltpu.VMEM((nq, tk, tq), jnp.int32),
                        pltpu.VMEM((nq, tk, tq), jnp.int16),
                        pltpu.VMEM((nq, tk, tq), jnp.int16),
                        pltpu.VMEM((nq, tk, tq), F32),
                        pltpu.VMEM((heads, tk, tq), F32),
                        pltpu.VMEM((heads, tk, tq), F32),
                        pltpu.VMEM((heads, 1, tq), F32),
                        pltpu.VMEM((heads, GROUP_DIM + DENOM_ROWS, tq), F32)],
        compiler_params=_cparams(("parallel", "arbitrary")),
        name="dsa",
    )(qt, k, vt, qit, ki, wt, out_gain)


def _rope_consts(rot_dim, period):
    half = rot_dim // 2
    inv = ROPE_THETA ** (-2.0 * jnp.arange(half, dtype=F32) / rot_dim)
    lane = jnp.arange(LANES) % period
    inv_l = jnp.where(lane < rot_dim, inv[lane % half], 0.0)
    sa = jnp.where((lane >= half) & (lane < rot_dim), 1.0, 0.0)
    sb = jnp.where(lane < half, -1.0, 0.0)
    return jnp.zeros((8, LANES), F32).at[0].set(inv_l).at[1].set(sa).at[2].set(sb)


def kernel(x, c, positions, w_ada, b_ada, norm1_g, ffn1_w1, ffn1_w3, ffn1_w2, norm2_g, w_in, gmlp_v_g, gmlp_ws,
           gmlp_b, q_norm_g, k_norm_g, out_norm_g, w_out, norm3_g, ffn2_w1, ffn2_w3, ffn2_w2):
    bsz, seq, d = x.shape
    depth = w_ada.shape[0]
    aw = gmlp_v_g.shape[1]
    bw = d - aw
    t = bsz * seq

    xf = x.reshape(t, d)
    pos = positions.reshape(t, 1)
    c_pad = jnp.zeros((8, d), F32).at[:bsz].set(c)
    rope_b = _rope_consts(B_ROT_DIM, GROUP_DIM)
    rope_i = _rope_consts(IDX_ROT_DIM, IDX_HEAD_DIM)
    idx_cols = IDX_HEADS * IDX_HEAD_DIM + IDX_HEAD_DIM + IDX_HEADS
    idx_pad = -idx_cols % 256

    for l in range(depth):
        mod = _ada(c_pad, w_ada[l], b_ada[l][None, :])[:bsz].reshape(bsz, 9, d)

        xf = _ffn(xf, mod, norm1_g[l][None, :], ffn1_w1[l].astype(BF16), ffn1_w3[l].astype(BF16),
                  ffn1_w2[l].astype(BF16), row0=0, seq=seq)

        w_l = w_in[l]
        wu = w_l[:, :aw].astype(BF16)
        wv = w_l[:, aw:2 * aw].astype(BF16)
        wqkv = w_l[:, 2 * aw:2 * aw + 3 * bw].astype(BF16)
        widx = jnp.pad(w_l[:, 2 * aw + 3 * bw:], ((0, 0), (0, idx_pad))).astype(BF16)
        g2 = norm2_g[l][None, :]
        bs = jnp.broadcast_to(gmlp_b[l][:, :, None], gmlp_b.shape[1:] + (GROUP_DIM,))

        ga, qt, k, vt, qit, ki, wt = _proj(
            xf, mod, g2, pos, rope_b, rope_i, wu, wv, gmlp_v_g[l][None, :], gmlp_ws[l], bs,
            out_norm_g[l][None, :aw], wqkv, jnp.stack([q_norm_g[l], k_norm_g[l]]), widx, row0=3, seq=seq)
        gb = _dsa(qt, k, vt, qit, ki, wt, out_norm_g[l][None, aw:], bsz=bsz, seq=seq)
        xf = _out(xf, mod, ga, gb, w_out[l].astype(BF16), row0=5, seq=seq)

        xf = _ffn(xf, mod, norm3_g[l][None, :], ffn2_w1[l].astype(BF16), ffn2_w3[l].astype(BF16),
                  ffn2_w2[l].astype(BF16), row0=6, seq=seq)
    return xf.reshape(bsz, seq, d)
```

```python
import functools

import jax
import jax.numpy as jnp
from jax import lax
from jax.experimental import pallas as pl
from jax.experimental.pallas import tpu as pltpu

F32 = jnp.float32
BF16 = jnp.bfloat16

CHUNK = 64
GMLP_WIN = 128
GROUP_DIM = 128
IDX_HEADS = 16
IDX_HEAD_DIM = 64
TOPK_MAX = 256
ROPE_THETA = 500000.0
B_ROT_DIM = GROUP_DIM // 4
IDX_ROT_DIM = IDX_HEAD_DIM // 4
EPS = 1e-6
NEG = -1e30

LANES = 128
I16_ROWS = 16
DENOM_ROWS = 16
LOG2E = 1.4426950408889634
NORM_ROWS = 16
NORM_UNROLL = 8
FFN_SUB = 256
PROJ_TN = 512
DSA_TILE = 256
SCORE_ROWS = 128
COUNT_CHAINS = 4
VMEM_LIMIT = 62 * 1024 * 1024


def _cparams(sem):
    return pltpu.CompilerParams(dimension_semantics=sem, vmem_limit_bytes=VMEM_LIMIT)


def _rms_mod_rows(x_ref, h_ref, g_ref, mod_ref, row0):
    gain = g_ref[...] * (1.0 + mod_ref[0, row0 + 1:row0 + 2, :])
    shift = mod_ref[0, row0:row0 + 1, :]

    def body(c, carry):
        r0 = pl.multiple_of(c * NORM_ROWS, NORM_ROWS)
        x = x_ref[pl.ds(r0, NORM_ROWS), :]
        ms = jnp.mean(x * x, axis=-1, keepdims=True)
        h_ref[pl.ds(r0, NORM_ROWS), :] = (x * lax.rsqrt(ms + EPS) * gain + shift).astype(h_ref.dtype)
        return carry

    lax.fori_loop(0, x_ref.shape[0] // NORM_ROWS, body, 0, unroll=NORM_UNROLL)


def _rope_tables(pos_ref, ropec_ref):
    ang = pos_ref[...].astype(F32) * ropec_ref[0:1, :]
    cos = jnp.cos(ang)
    sin = jnp.sin(ang)
    return cos, sin * ropec_ref[1:2, :], sin * ropec_ref[2:3, :]


def _rope_apply(x, cos, sa, sb, half):
    return x * cos + pltpu.roll(x, half, 1) * sa + pltpu.roll(x, LANES - half, 1) * sb


def _ada_kernel(c_ref, w_ref, b_ref, o_ref):
    c = c_ref[...]
    act = c * jax.nn.sigmoid(c)
    o_ref[...] = jnp.dot(act, w_ref[...], preferred_element_type=F32) + b_ref[...]


def _ada(c_pad, w_ada, b_ada, tn=1024):
    rows, d = c_pad.shape
    n = w_ada.shape[1]
    return pl.pallas_call(
        _ada_kernel,
        grid=(n // tn,),
        in_specs=[pl.BlockSpec((rows, d), lambda j: (0, 0)),
                  pl.BlockSpec((d, tn), lambda j: (0, j)),
                  pl.BlockSpec((1, tn), lambda j: (0, j))],
        out_specs=pl.BlockSpec((rows, tn), lambda j: (0, j)),
        out_shape=jax.ShapeDtypeStruct((rows, n), F32),
        compiler_params=_cparams(("arbitrary",)),
        name="ada",
    )(c_pad, w_ada, b_ada)


def _ffn_kernel(x_ref, mod_ref, g_ref, w1_ref, w3_ref, w2_ref, o_ref, h_ref, *, row0):
    j = pl.program_id(1)

    @pl.when(j == 0)
    def _():
        _rms_mod_rows(x_ref, h_ref, g_ref, mod_ref, row0)
        o_ref[...] = jnp.zeros_like(o_ref)

    h = h_ref[...]
    tf = w1_ref.shape[1]
    ab = [(jnp.dot(h, w1_ref[:, c0:c0 + FFN_SUB], preferred_element_type=F32),
           jnp.dot(h, w3_ref[:, c0:c0 + FFN_SUB], preferred_element_type=F32)) for c0 in range(0, tf, FFN_SUB)]
    upd = None
    for idx, (a, b) in enumerate(ab):
        act = (a * jax.nn.sigmoid(a) * b).astype(BF16)
        part = jnp.dot(act, w2_ref[idx * FFN_SUB:(idx + 1) * FFN_SUB, :], preferred_element_type=F32)
        upd = part if upd is None else upd + part
    o_ref[...] += upd

    @pl.when(j == pl.num_programs(1) - 1)
    def _():
        o_ref[...] = x_ref[...] + (0.5 * mod_ref[0, row0 + 2:row0 + 3, :]) * o_ref[...]


def _ffn(x, mod, gain, w1, w3, w2, *, row0, seq, tm=1024, tf=512):
    t, d = x.shape
    f = w1.shape[1]
    bpb = seq // tm
    return pl.pallas_call(
        functools.partial(_ffn_kernel, row0=row0),
        grid=(t // tm, f // tf),
        in_specs=[pl.BlockSpec((tm, d), lambda i, j: (i, 0)),
                  pl.BlockSpec((1,) + mod.shape[1:], lambda i, j: (i // bpb, 0, 0)),
                  pl.BlockSpec((1, d), lambda i, j: (0, 0)),
                  pl.BlockSpec((d, tf), lambda i, j: (0, j)),
                  pl.BlockSpec((d, tf), lambda i, j: (0, j)),
                  pl.BlockSpec((tf, d), lambda i, j: (j, 0))],
        out_specs=pl.BlockSpec((tm, d), lambda i, j: (i, 0)),
        out_shape=jax.ShapeDtypeStruct((t, d), F32),
        scratch_shapes=[pltpu.VMEM((tm, d), BF16)],
        compiler_params=_cparams(("parallel", "arbitrary")),
        name="ffn",
    )(x, mod, gain, w1, w3, w2)


def _out_kernel(x_ref, mod_ref, a_ref, b_ref, wo_ref, o_ref, *, row0):
    half = a_ref.shape[1]
    mixed = jnp.dot(a_ref[...], wo_ref[0:half, :], preferred_element_type=F32)
    mixed = mixed + jnp.dot(b_ref[...], wo_ref[half:, :], preferred_element_type=F32)
    o_ref[...] = x_ref[...] + mod_ref[0, row0:row0 + 1, :] * mixed


def _out(x, mod, ga, gb, w_out, *, row0, seq, tm=512):
    t, d = x.shape
    bpb = seq // tm
    return pl.pallas_call(
        functools.partial(_out_kernel, row0=row0),
        grid=(t // tm,),
        in_specs=[pl.BlockSpec((tm, d), lambda i: (i, 0)),
                  pl.BlockSpec((1,) + mod.shape[1:], lambda i: (i // bpb, 0, 0)),
                  pl.BlockSpec((tm, ga.shape[1]), lambda i: (i, 0)),
                  pl.BlockSpec((tm, gb.shape[1]), lambda i: (i, 0)),
                  pl.BlockSpec(w_out.shape, lambda i: (0, 0), pipeline_mode=pl.Buffered(1))],
        out_specs=pl.BlockSpec((tm, d), lambda i: (i, 0)),
        out_shape=jax.ShapeDtypeStruct((t, d), F32),
        compiler_params=_cparams(("parallel",)),
        name="out",
    )(x, mod, ga, gb, w_out)


def _gmlp_cols(h_ref, wu_ref, wv_ref, vg_ref, ws_ref, bs_ref, og_ref, o_ref):
    tm = h_ref.shape[0]
    ri = lax.broadcasted_iota(jnp.int32, (GMLP_WIN, GMLP_WIN), 0) // CHUNK
    ci = lax.broadcasted_iota(jnp.int32, (GMLP_WIN, GMLP_WIN), 1) // CHUNK
    for c0 in range(0, wu_ref.shape[1], PROJ_TN):
        u = jax.nn.gelu(jnp.dot(h_ref[...], wu_ref[:, c0:c0 + PROJ_TN], preferred_element_type=F32))
        v = jax.nn.gelu(jnp.dot(h_ref[...], wv_ref[:, c0:c0 + PROJ_TN], preferred_element_type=F32))
        for sub in range(PROJ_TN // GROUP_DIM):
            lo = c0 + sub * GROUP_DIM
            g = lo // GROUP_DIM
            vg = v[:, sub * GROUP_DIM:(sub + 1) * GROUP_DIM]
            mu = jnp.mean(vg, axis=-1, keepdims=True)
            vc = vg - mu
            var = jnp.mean(vc * vc, axis=-1, keepdims=True)
            y = (vc * lax.rsqrt(var + EPS) * vg_ref[:, lo:lo + GROUP_DIM]).astype(BF16)
            w = jnp.where(ci <= ri, ws_ref[g], 0.0).astype(BF16)
            for win in range(tm // GMLP_WIN):
                r0 = win * GMLP_WIN
                mixed = jnp.dot(w, y[r0:r0 + GMLP_WIN], preferred_element_type=F32) + bs_ref[g]
                o = u[r0:r0 + GMLP_WIN, sub * GROUP_DIM:(sub + 1) * GROUP_DIM] * mixed
                ms = jnp.mean(o * o, axis=-1, keepdims=True)
                o = o * lax.rsqrt(ms + EPS) * og_ref[:, lo:lo + GROUP_DIM]
                o_ref[r0:r0 + GMLP_WIN, lo:lo + GROUP_DIM] = o.astype(o_ref.dtype)


def _qkv_cols(h_ref, pos_ref, ropec_ref, w_ref, qkg_ref, qt_ref, k_ref, vt_ref, attn_scale):
    cos, sa, sb = _rope_tables(pos_ref, ropec_ref)
    bw = k_ref.shape[1]

    def normed(xh, gain):
        ms = jnp.mean(xh * xh, axis=-1, keepdims=True)
        return _rope_apply(xh * lax.rsqrt(ms + EPS) * gain, cos, sa, sb, B_ROT_DIM // 2)

    for c0 in range(0, w_ref.shape[1], PROJ_TN):
        acc = jnp.dot(h_ref[...], w_ref[:, c0:c0 + PROJ_TN], preferred_element_type=F32)
        for sub in range(PROJ_TN // GROUP_DIM):
            xh = acc[:, sub * GROUP_DIM:(sub + 1) * GROUP_DIM]
            col = c0 + sub * GROUP_DIM
            if col < bw:
                qh = normed(xh, qkg_ref[0:1, :]) * attn_scale
                qt_ref[col:col + GROUP_DIM, :] = jnp.transpose(qh).astype(qt_ref.dtype)
            elif col < 2 * bw:
                k_ref[:, col - bw:col - bw + GROUP_DIM] = normed(xh, qkg_ref[1:2, :]).astype(k_ref.dtype)
            else:
                vh = jnp.transpose(xh).astype(vt_ref.dtype)
                for kb in range(vt_ref.shape[0]):
                    vt_ref[kb, col - 2 * bw:col - 2 * bw + GROUP_DIM, :] = vh[:, kb * DSA_TILE:(kb + 1) * DSA_TILE]


def _idx_cols(h_ref, pos_ref, ropec_ref, w_ref, qit_ref, ki_ref, wt_ref, w_scale):
    cos, sa, sb = _rope_tables(pos_ref, ropec_ref)
    half = IDX_ROT_DIM // 2
    q_cols = IDX_HEADS * IDX_HEAD_DIM

    for c0 in range(0, w_ref.shape[1], PROJ_TN):
        tn = min(PROJ_TN, w_ref.shape[1] - c0)
        acc = jnp.dot(h_ref[...], w_ref[:, c0:c0 + tn], preferred_element_type=F32)
        if c0 < q_cols:
            for c in range(tn // LANES):
                xt = jnp.transpose(_rope_apply(acc[:, c * LANES:(c + 1) * LANES], cos, sa, sb, half))
                for sub in range(LANES // IDX_HEAD_DIM):
                    hd = (c0 + c * LANES) // IDX_HEAD_DIM + sub
                    qit_ref[hd] = xt[sub * IDX_HEAD_DIM:(sub + 1) * IDX_HEAD_DIM, :].astype(qit_ref.dtype)
        else:
            x = acc[:, 0:LANES]
            ki_ref[...] = _rope_apply(x, cos, sa, sb, half)[:, 0:IDX_HEAD_DIM].astype(ki_ref.dtype)
            wt_ref[...] = jnp.transpose(x * w_scale)[IDX_HEAD_DIM:IDX_HEAD_DIM + IDX_HEADS, :]


def _proj_kernel(x_ref, mod_ref, g_ref, pos_ref, ropeb_ref, ropei_ref,
                 wu_ref, wv_ref, vg_ref, ws_ref, bs_ref, og_ref, wqkv_ref, qkg_ref, widx_ref,
                 ga_ref, qt_ref, k_ref, vt_ref, qit_ref, ki_ref, wt_ref, h_ref, *, row0, attn_scale, w_scale):
    x = x_ref[...]
    ms = jnp.mean(x * x, axis=-1, keepdims=True)
    h = x * lax.rsqrt(ms + EPS) * g_ref[...]
    h_ref[...] = (h * (1.0 + mod_ref[0, row0 + 1:row0 + 2, :]) + mod_ref[0, row0:row0 + 1, :]).astype(BF16)
    _idx_cols(h_ref, pos_ref, ropei_ref, widx_ref, qit_ref, ki_ref, wt_ref, w_scale)
    _qkv_cols(h_ref, pos_ref, ropeb_ref, wqkv_ref, qkg_ref, qt_ref, k_ref, vt_ref, attn_scale)
    _gmlp_cols(h_ref, wu_ref, wv_ref, vg_ref, ws_ref, bs_ref, og_ref, ga_ref)


def _proj(x, mod, gain, pos, rope_b, rope_i, wu, wv, v_gain, ws, bs, out_gain, wqkv, qk_gain, widx,
          *, row0, seq, tm=512):
    t, d = x.shape
    aw = wu.shape[1]
    bw = wqkv.shape[1] // 3
    bpb = seq // tm
    once = pl.Buffered(1)

    def resident(a):
        return pl.BlockSpec(a.shape, lambda i: (0,) * a.ndim, pipeline_mode=once)

    kern = functools.partial(_proj_kernel, row0=row0, attn_scale=float(GROUP_DIM) ** -0.5 * LOG2E,
                             w_scale=float(IDX_HEADS) ** -0.5 * float(IDX_HEAD_DIM) ** -0.5)
    return pl.pallas_call(
        kern,
        grid=(t // tm,),
        in_specs=[pl.BlockSpec((tm, d), lambda i: (i, 0)),
                  pl.BlockSpec((1,) + mod.shape[1:], lambda i: (i // bpb, 0, 0)),
                  pl.BlockSpec((1, d), lambda i: (0, 0)),
                  pl.BlockSpec((tm, 1), lambda i: (i, 0)),
                  resident(rope_b), resident(rope_i),
                  resident(wu), resident(wv), resident(v_gain), resident(ws), resident(bs), resident(out_gain),
                  resident(wqkv), resident(qk_gain), resident(widx)],
        out_specs=[pl.BlockSpec((tm, aw), lambda i: (i, 0)),
                   pl.BlockSpec((bw, tm), lambda i: (0, i)),
                   pl.BlockSpec((tm, bw), lambda i: (i, 0)),
                   pl.BlockSpec((tm // DSA_TILE, bw, DSA_TILE), lambda i: (i, 0, 0)),
                   pl.BlockSpec((IDX_HEADS, IDX_HEAD_DIM, tm), lambda i: (0, 0, i)),
                   pl.BlockSpec((tm, IDX_HEAD_DIM), lambda i: (i, 0)),
                   pl.BlockSpec((IDX_HEADS, tm), lambda i: (0, i))],
        out_shape=[jax.ShapeDtypeStruct((t, aw), BF16),
                   jax.ShapeDtypeStruct((bw, t), BF16),
                   jax.ShapeDtypeStruct((t, bw), BF16),
                   jax.ShapeDtypeStruct((t // DSA_TILE, bw, DSA_TILE), BF16),
                   jax.ShapeDtypeStruct((IDX_HEADS, IDX_HEAD_DIM, t), BF16),
                   jax.ShapeDtypeStruct((t, IDX_HEAD_DIM), BF16),
                   jax.ShapeDtypeStruct((IDX_HEADS, t), F32)],
        scratch_shapes=[pltpu.VMEM((tm, d), BF16)],
        compiler_params=_cparams(("parallel",)),
        name="proj",
    )(x, mod, gain, pos, rope_b, rope_i, wu, wv, v_gain, ws, bs, out_gain, wqkv, qk_gain, widx)


def _dsa_kernel(qt_ref, k_ref, vt_ref, qit_ref, ki_ref, wt_ref, og_ref, o_ref,
                key_ref, hi_ref, lo_ref, bias_ref, sa_ref, sb_ref, m_ref, acc_ref, *, topk):
    _, tk, tq = key_ref.shape
    jq = pl.program_id(1)
    nt = jq + 1
    heads = qt_ref.shape[0] // GROUP_DIM
    q_chunk = (jq * tq + lax.broadcasted_iota(jnp.int32, (1, tq), 1)) // CHUNK

    def score_tile(kt, carry):
        k0 = pl.multiple_of(kt * tk, tk)
        for r0 in range(0, tk, SCORE_ROWS):
            ki = ki_ref[pl.ds(k0 + r0, SCORE_ROWS), :]
            acc = jnp.zeros((SCORE_ROWS, tq), F32)
            for hd in range(IDX_HEADS):
                logit = jnp.dot(ki, qit_ref[hd], preferred_element_type=F32)
                acc = acc + jnp.maximum(logit, 0.0) * wt_ref[hd:hd + 1, :]
            k_chunk = (k0 + r0 + lax.broadcasted_iota(jnp.int32, (SCORE_ROWS, 1), 0)) // CHUNK
            bits = pltpu.bitcast(jnp.where(k_chunk <= q_chunk, acc, NEG), jnp.int32)
            key = jnp.where(bits < 0, bits ^ 0x7FFFFFFF, bits)
            key_ref[kt, r0:r0 + SCORE_ROWS, :] = key
            hi_ref[kt, r0:r0 + SCORE_ROWS, :] = (key >> 16).astype(jnp.int16)
            lo_ref[kt, r0:r0 + SCORE_ROWS, :] = ((key & 0xFFFF) - 2 ** 15).astype(jnp.int16)
        return carry

    lax.fori_loop(0, nt, score_tile, 0)

    one16 = jnp.ones((I16_ROWS, tq), jnp.int16)
    zero16 = jnp.zeros((I16_ROWS, tq), jnp.int16)

    def count16(parts):
        return jnp.sum(sum(p.astype(jnp.int32) for p in parts), axis=0, keepdims=True)

    def select16(src_ref, need):
        def bit_step(i, u):
            cand_u = u | (jnp.int32(1) << (15 - i))
            cand = jnp.broadcast_to((cand_u - 2 ** 15).astype(jnp.int16), (I16_ROWS, tq))

            def count_tile(kt, parts):
                parts = list(parts)
                for r in range(tk // I16_ROWS):
                    ge = src_ref[kt, r * I16_ROWS:(r + 1) * I16_ROWS, :] >= cand
                    parts[r % COUNT_CHAINS] = parts[r % COUNT_CHAINS] + jnp.where(ge, one16, zero16)
                return tuple(parts)

            parts = lax.fori_loop(0, nt, count_tile, (zero16,) * COUNT_CHAINS)
            return jnp.where(count16(parts) >= need, cand_u, u)

        return lax.fori_loop(0, 16, bit_step, jnp.zeros((1, tq), jnp.int32))

    u_hi = select16(hi_ref, topk)
    t_hi = jnp.broadcast_to((u_hi - 2 ** 15).astype(jnp.int16), (I16_ROWS, tq))

    def split_tile(kt, parts):
        parts = list(parts)
        for r in range(tk // I16_ROWS):
            rows = slice(r * I16_ROWS, (r + 1) * I16_ROWS)
            hi = hi_ref[kt, rows, :]
            parts[r % COUNT_CHAINS] = parts[r % COUNT_CHAINS] + jnp.where(hi > t_hi, one16, zero16)
            lo_ref[kt, rows, :] = jnp.where(hi == t_hi, lo_ref[kt, rows, :], jnp.full_like(zero16, -(2 ** 15)))
        return tuple(parts)

    above = count16(lax.fori_loop(0, nt, split_tile, (zero16,) * COUNT_CHAINS))
    u_lo = select16(lo_ref, topk - above)
    thr = (u_hi - 2 ** 15) * 2 ** 16 + u_lo

    thr8 = jnp.broadcast_to(thr, (8, tq))

    def bias_tile(kt, parts):
        parts = list(parts)
        for r in range(tk // 8):
            ge = key_ref[kt, r * 8:(r + 1) * 8, :] >= thr8
            bias_ref[kt, r * 8:(r + 1) * 8, :] = jnp.where(ge, 0.0, NEG)
            parts[r % COUNT_CHAINS] = parts[r % COUNT_CHAINS] + jnp.where(ge, 1.0, 0.0)
        return tuple(parts)

    parts = lax.fori_loop(0, nt, bias_tile, (jnp.zeros((8, tq), F32),) * COUNT_CHAINS)
    excess = jnp.sum(sum(parts), axis=0, keepdims=True) - float(topk)

    @pl.when(jnp.max(excess) > 0.0)
    def _():
        def tied(kt):
            return key_ref[kt] == thr

        def index(kt):
            return kt * tk + lax.broadcasted_iota(jnp.int32, (tk, tq), 0)

        def count(pred_of_tile):
            def body(kt, acc):
                return acc + jnp.sum(jnp.where(pred_of_tile(kt), 1.0, 0.0), axis=0, keepdims=True)
            return lax.fori_loop(0, nt, body, jnp.zeros((1, tq), F32))

        keep = count(tied) - excess
        nbits = (tk * key_ref.shape[0] - 1).bit_length()

        def bit_step(i, last):
            cand = last | (jnp.int32(1) << (nbits - 1 - i))
            below = count(lambda kt: tied(kt) & (index(kt) < cand))
            return jnp.where(below < keep, cand, last)

        last = lax.fori_loop(0, nbits, bit_step, jnp.zeros((1, tq), jnp.int32))

        def drop_tile(kt, carry):
            bias_ref[kt] = jnp.where(tied(kt) & (index(kt) > last), NEG, bias_ref[kt])
            return carry

        lax.fori_loop(0, nt, drop_tile, 0)

    k_chunk = (jq * tk + lax.broadcasted_iota(jnp.int32, (tk, tq), 0)) // CHUNK
    bias_ref[jq] = jnp.where(k_chunk <= q_chunk, bias_ref[jq], NEG)

    m_ref[...] = jnp.full(m_ref.shape, NEG, F32)
    acc_ref[...] = jnp.zeros(acc_ref.shape, F32)
    ones_rows = jnp.ones((DENOM_ROWS, tk), BF16)

    def score_head(kt, s_ref, hd):
        k0 = pl.multiple_of(kt * tk, tk)
        lo = hd * GROUP_DIM
        s_ref[hd] = jnp.dot(k_ref[pl.ds(k0, tk), lo:lo + GROUP_DIM], qt_ref[lo:lo + GROUP_DIM, :],
                            preferred_element_type=F32)

    def softmax_pv_head(kt, s_ref, hd):
        lo = hd * GROUP_DIM
        ps, alphas = [], []
        for c0 in range(0, tq, LANES):
            s = s_ref[hd, :, c0:c0 + LANES] + bias_ref[kt, :, c0:c0 + LANES]
            m_prev = m_ref[hd, :, c0:c0 + LANES]
            m_new = jnp.maximum(m_prev, jnp.max(s, axis=0, keepdims=True))
            m_ref[hd, :, c0:c0 + LANES] = m_new
            alphas.append(jnp.exp2(m_prev - m_new))
            ps.append(jnp.exp2(s - m_new).astype(BF16))
        alpha = jnp.concatenate(alphas, axis=1)
        vt1 = jnp.concatenate([vt_ref[kt, lo:lo + GROUP_DIM, :], ones_rows], axis=0)
        acc_ref[hd] = alpha * acc_ref[hd] + jnp.dot(vt1, jnp.concatenate(ps, axis=1), preferred_element_type=F32)

    def scores_into(kt, s_ref):
        for hd in range(heads):
            score_head(kt, s_ref, hd)

    def softmax_pv(kt, s_ref, next_kt, next_ref):
        for hd in range(heads):
            score_head(next_kt, next_ref, hd)
            softmax_pv_head(kt, s_ref, hd)

    @pl.when(nt % 2 == 1)
    def _():
        bias_ref[nt] = jnp.full((tk, tq), NEG, F32)

    scores_into(0, sa_ref)

    def attn_pair(j, carry):
        t0 = 2 * j
        softmax_pv(t0, sa_ref, t0 + 1, sb_ref)
        softmax_pv(t0 + 1, sb_ref, jnp.minimum(t0 + 2, nt - 1), sa_ref)
        return carry

    lax.fori_loop(0, (nt + 1) // 2, attn_pair, 0)

    for hd in range(heads):
        lo = hd * GROUP_DIM
        o = jnp.transpose(acc_ref[hd, 0:GROUP_DIM, :] / acc_ref[hd, GROUP_DIM:GROUP_DIM + 1, :])
        ms = jnp.mean(o * o, axis=-1, keepdims=True)
        o = o * lax.rsqrt(ms + EPS) * og_ref[:, lo:lo + GROUP_DIM]
        o_ref[:, lo:lo + GROUP_DIM] = o.astype(o_ref.dtype)


def _dsa(qt, k, vt, qit, ki, wt, out_gain, *, bsz, seq):
    bw, t = qt.shape
    heads = bw // GROUP_DIM
    tq = tk = DSA_TILE
    nq = seq // tq
    topk = min(TOPK_MAX, seq // 4)
    assert tq >= topk and tq % CHUNK == 0 and nq % 2 == 0
    once = pl.Buffered(1)
    return pl.pallas_call(
        functools.partial(_dsa_kernel, topk=topk),
        grid=(bsz, nq),
        in_specs=[pl.BlockSpec((bw, tq), lambda b, j: (0, b * nq + j)),
                  pl.BlockSpec((seq, bw), lambda b, j: (b, 0), pipeline_mode=once),
                  pl.BlockSpec((nq, bw, tk), lambda b, j: (b, 0, 0), pipeline_mode=once),
                  pl.BlockSpec((IDX_HEADS, IDX_HEAD_DIM, tq), lambda b, j: (0, 0, b * nq + j)),
                  pl.BlockSpec((seq, IDX_HEAD_DIM), lambda b, j: (b, 0), pipeline_mode=once),
                  pl.BlockSpec((IDX_HEADS, tq), lambda b, j: (0, b * nq + j)),
                  pl.BlockSpec((1, bw), lambda b, j: (0, 0))],
        out_specs=pl.BlockSpec((tq, bw), lambda b, j: (b * nq + j, 0)),
        out_shape=jax.ShapeDtypeStruct((t, bw), BF16),
        scratch_shapes=[pltpu.VMEM((nq, tk, tq), jnp.int32),
                        pltpu.VMEM((nq, tk, tq), jnp.int16),
                        pltpu.VMEM((nq, tk, tq), jnp.int16),
                        pltpu.VMEM((nq, tk, tq), F32),
                        pltpu.VMEM((heads, tk, tq), F32),
                        pltpu.VMEM((heads, tk, tq), F32),
                        pltpu.VMEM((heads, 1, tq), F32),
                        pltpu.VMEM((heads, GROUP_DIM + DENOM_ROWS, tq), F32)],
        compiler_params=_cparams(("parallel", "arbitrary")),
        name="dsa",
    )(qt, k, vt, qit, ki, wt, out_gain)


def _rope_consts(rot_dim, period):
    half = rot_dim // 2
    inv = ROPE_THETA ** (-2.0 * jnp.arange(half, dtype=F32) / rot_dim)
    lane = jnp.arange(LANES) % period
    inv_l = jnp.where(lane < rot_dim, inv[lane % half], 0.0)
    sa = jnp.where((lane >= half) & (lane < rot_dim), 1.0, 0.0)
    sb = jnp.where(lane < half, -1.0, 0.0)
    return jnp.zeros((8, LANES), F32).at[0].set(inv_l).at[1].set(sa).at[2].set(sb)


def kernel(x, c, positions, w_ada, b_ada, norm1_g, ffn1_w1, ffn1_w3, ffn1_w2, norm2_g, w_in, gmlp_v_g, gmlp_ws,
           gmlp_b, q_norm_g, k_norm_g, out_norm_g, w_out, norm3_g, ffn2_w1, ffn2_w3, ffn2_w2):
    bsz, seq, d = x.shape
    depth = w_ada.shape[0]
    aw = gmlp_v_g.shape[1]
    bw = d - aw
    t = bsz * seq

    xf = x.reshape(t, d)
    pos = positions.reshape(t, 1)
    c_pad = jnp.zeros((8, d), F32).at[:bsz].set(c)
    rope_b = _rope_consts(B_ROT_DIM, GROUP_DIM)
    rope_i = _rope_consts(IDX_ROT_DIM, IDX_HEAD_DIM)
    idx_cols = IDX_HEADS * IDX_HEAD_DIM + IDX_HEAD_DIM + IDX_HEADS
    idx_pad = -idx_cols % 256

    for l in range(depth):
        mod = _ada(c_pad, w_ada[l], b_ada[l][None, :])[:bsz].reshape(bsz, 9, d)

        xf = _ffn(xf, mod, norm1_g[l][None, :], ffn1_w1[l].astype(BF16), ffn1_w3[l].astype(BF16),
                  ffn1_w2[l].astype(BF16), row0=0, seq=seq)

        w_l = w_in[l]
        wu = w_l[:, :aw].astype(BF16)
        wv = w_l[:, aw:2 * aw].astype(BF16)
        wqkv = w_l[:, 2 * aw:2 * aw + 3 * bw].astype(BF16)
        widx = jnp.pad(w_l[:, 2 * aw + 3 * bw:], ((0, 0), (0, idx_pad))).astype(BF16)
        g2 = norm2_g[l][None, :]
        bs = jnp.broadcast_to(gmlp_b[l][:, :, None], gmlp_b.shape[1:] + (GROUP_DIM,))

        ga, qt, k, vt, qit, ki, wt = _proj(
            xf, mod, g2, pos, rope_b, rope_i, wu, wv, gmlp_v_g[l][None, :], gmlp_ws[l], bs,
            out_norm_g[l][None, :aw], wqkv, jnp.stack([q_norm_g[l], k_norm_g[l]]), widx, row0=3, seq=seq)
        gb = _dsa(qt, k, vt, qit, ki, wt, out_norm_g[l][None, aw:], bsz=bsz, seq=seq)
        xf = _out(xf, mod, ga, gb, w_out[l].astype(BF16), row0=5, seq=seq)

        xf = _ffn(xf, mod, norm3_g[l][None, :], ffn2_w1[l].astype(BF16), ffn2_w3[l].astype(BF16),
                  ffn2_w2[l].astype(BF16), row0=6, seq=seq)
    return xf.reshape(bsz, seq, d)
```

```python
import functools

import jax
import jax.numpy as jnp
import numpy as np
from jax import lax
from jax.experimental import pallas as pl
from jax.experimental.pallas import tpu as pltpu

F32 = jnp.float32
BF16 = jnp.bfloat16

CHUNK = 64
GMLP_WIN = 128
GROUP_DIM = 128
IDX_HEADS = 16
IDX_HEAD_DIM = 64
TOPK_MAX = 256
ROPE_THETA = 500000.0
B_ROT_DIM = GROUP_DIM // 4
IDX_ROT_DIM = IDX_HEAD_DIM // 4
EPS = 1e-6
NEG = -1e30

LANES = 128
I16_ROWS = 16
DENOM_ROWS = 16
LOG2E = 1.4426950408889634
NORM_ROWS = 16
NORM_UNROLL = 8
FFN_SUB = 256
PROJ_TN = 512
DSA_TILE = 256
SCORE_ROWS = 64
COUNT_CHAINS = 4
VMEM_LIMIT = 62 * 1024 * 1024


def _cparams(sem):
    return pltpu.CompilerParams(dimension_semantics=sem, vmem_limit_bytes=VMEM_LIMIT)


def _rms_mod_rows(x_ref, h_ref, g_ref, mod_ref, row0):
    gain = g_ref[...] * (1.0 + mod_ref[0, row0 + 1:row0 + 2, :])
    shift = mod_ref[0, row0:row0 + 1, :]

    def body(c, carry):
        r0 = pl.multiple_of(c * NORM_ROWS, NORM_ROWS)
        x = x_ref[pl.ds(r0, NORM_ROWS), :]
        ms = jnp.mean(x * x, axis=-1, keepdims=True)
        h_ref[pl.ds(r0, NORM_ROWS), :] = (x * lax.rsqrt(ms + EPS) * gain + shift).astype(h_ref.dtype)
        return carry

    lax.fori_loop(0, x_ref.shape[0] // NORM_ROWS, body, 0, unroll=NORM_UNROLL)


def _rope_tables(pos_ref, ropec_ref):
    ang = pos_ref[...].astype(F32) * ropec_ref[0:1, :]
    cos = jnp.cos(ang)
    sin = jnp.sin(ang)
    return cos, sin * ropec_ref[1:2, :], sin * ropec_ref[2:3, :]


def _rope_apply(x, cos, sa, sb, half):
    return x * cos + pltpu.roll(x, half, 1) * sa + pltpu.roll(x, LANES - half, 1) * sb


def _ada_kernel(c_ref, w_ref, b_ref, o_ref):
    c = c_ref[...]
    act = c * jax.nn.sigmoid(c)
    o_ref[...] = jnp.dot(act, w_ref[...], preferred_element_type=F32) + b_ref[...]


def _ada(c_pad, w_ada, b_ada, tn=1024):
    rows, d = c_pad.shape
    n = w_ada.shape[1]
    return pl.pallas_call(
        _ada_kernel,
        grid=(n // tn,),
        in_specs=[pl.BlockSpec((rows, d), lambda j: (0, 0)),
                  pl.BlockSpec((d, tn), lambda j: (0, j)),
                  pl.BlockSpec((1, tn), lambda j: (0, j))],
        out_specs=pl.BlockSpec((rows, tn), lambda j: (0, j)),
        out_shape=jax.ShapeDtypeStruct((rows, n), F32),
        compiler_params=_cparams(("arbitrary",)),
        name="ada",
    )(c_pad, w_ada, b_ada)


def _ffn_kernel(x_ref, mod_ref, g_ref, w1_ref, w3_ref, w2_ref, o_ref, h_ref, *, row0):
    j = pl.program_id(1)

    @pl.when(j == 0)
    def _():
        _rms_mod_rows(x_ref, h_ref, g_ref, mod_ref, row0)
        o_ref[...] = jnp.zeros_like(o_ref)

    h = h_ref[...]
    tf = w1_ref.shape[1]
    ab = [(jnp.dot(h, w1_ref[:, c0:c0 + FFN_SUB], preferred_element_type=F32),
           jnp.dot(h, w3_ref[:, c0:c0 + FFN_SUB], preferred_element_type=F32)) for c0 in range(0, tf, FFN_SUB)]
    upd = None
    for idx, (a, b) in enumerate(ab):
        act = (a * jax.nn.sigmoid(a) * b).astype(BF16)
        part = jnp.dot(act, w2_ref[idx * FFN_SUB:(idx + 1) * FFN_SUB, :], preferred_element_type=F32)
        upd = part if upd is None else upd + part
    o_ref[...] += upd

    @pl.when(j == pl.num_programs(1) - 1)
    def _():
        o_ref[...] = x_ref[...] + (0.5 * mod_ref[0, row0 + 2:row0 + 3, :]) * o_ref[...]


def _ffn(x, mod, gain, w1, w3, w2, *, row0, seq, tm=1024, tf=512):
    t, d = x.shape
    f = w1.shape[1]
    bpb = seq // tm
    return pl.pallas_call(
        functools.partial(_ffn_kernel, row0=row0),
        grid=(t // tm, f // tf),
        in_specs=[pl.BlockSpec((tm, d), lambda i, j: (i, 0)),
                  pl.BlockSpec((1,) + mod.shape[1:], lambda i, j: (i // bpb, 0, 0)),
                  pl.BlockSpec((1, d), lambda i, j: (0, 0)),
                  pl.BlockSpec((d, tf), lambda i, j: (0, j)),
                  pl.BlockSpec((d, tf), lambda i, j: (0, j)),
                  pl.BlockSpec((tf, d), lambda i, j: (j, 0))],
        out_specs=pl.BlockSpec((tm, d), lambda i, j: (i, 0)),
        out_shape=jax.ShapeDtypeStruct((t, d), F32),
        scratch_shapes=[pltpu.VMEM((tm, d), BF16)],
        compiler_params=_cparams(("parallel", "arbitrary")),
        name="ffn",
    )(x, mod, gain, w1, w3, w2)


def _out_kernel(x_ref, mod_ref, a_ref, b_ref, wo_ref, o_ref, *, row0):
    half = a_ref.shape[1]
    mixed = jnp.dot(a_ref[...], wo_ref[0:half, :], preferred_element_type=F32)
    mixed = mixed + jnp.dot(b_ref[...], wo_ref[half:, :], preferred_element_type=F32)
    o_ref[...] = x_ref[...] + mod_ref[0, row0:row0 + 1, :] * mixed


def _out(x, mod, ga, gb, w_out, *, row0, seq, tm=512):
    t, d = x.shape
    bpb = seq // tm
    return pl.pallas_call(
        functools.partial(_out_kernel, row0=row0),
        grid=(t // tm,),
        in_specs=[pl.BlockSpec((tm, d), lambda i: (i, 0)),
                  pl.BlockSpec((1,) + mod.shape[1:], lambda i: (i // bpb, 0, 0)),
                  pl.BlockSpec((tm, ga.shape[1]), lambda i: (i, 0)),
                  pl.BlockSpec((tm, gb.shape[1]), lambda i: (i, 0)),
                  pl.BlockSpec(w_out.shape, lambda i: (0, 0), pipeline_mode=pl.Buffered(1))],
        out_specs=pl.BlockSpec((tm, d), lambda i: (i, 0)),
        out_shape=jax.ShapeDtypeStruct((t, d), F32),
        compiler_params=_cparams(("parallel",)),
        name="out",
    )(x, mod, ga, gb, w_out)


def _gmlp_cols(h_ref, wu_ref, wv_ref, vg_ref, ws_ref, bs_ref, og_ref, o_ref):
    tm = h_ref.shape[0]
    ri = lax.broadcasted_iota(jnp.int32, (GMLP_WIN, GMLP_WIN), 0) // CHUNK
    ci = lax.broadcasted_iota(jnp.int32, (GMLP_WIN, GMLP_WIN), 1) // CHUNK
    for c0 in range(0, wu_ref.shape[1], PROJ_TN):
        u = jax.nn.gelu(jnp.dot(h_ref[...], wu_ref[:, c0:c0 + PROJ_TN], preferred_element_type=F32))
        v = jax.nn.gelu(jnp.dot(h_ref[...], wv_ref[:, c0:c0 + PROJ_TN], preferred_element_type=F32))
        for sub in range(PROJ_TN // GROUP_DIM):
            lo = c0 + sub * GROUP_DIM
            g = lo // GROUP_DIM
            vg = v[:, sub * GROUP_DIM:(sub + 1) * GROUP_DIM]
            mu = jnp.mean(vg, axis=-1, keepdims=True)
            vc = vg - mu
            var = jnp.mean(vc * vc, axis=-1, keepdims=True)
            y = (vc * lax.rsqrt(var + EPS) * vg_ref[:, lo:lo + GROUP_DIM]).astype(BF16)
            w = jnp.where(ci <= ri, ws_ref[g], 0.0).astype(BF16)
            for win in range(tm // GMLP_WIN):
                r0 = win * GMLP_WIN
                mixed = jnp.dot(w, y[r0:r0 + GMLP_WIN], preferred_element_type=F32) + bs_ref[g]
                o = u[r0:r0 + GMLP_WIN, sub * GROUP_DIM:(sub + 1) * GROUP_DIM] * mixed
                ms = jnp.mean(o * o, axis=-1, keepdims=True)
                o = o * lax.rsqrt(ms + EPS) * og_ref[:, lo:lo + GROUP_DIM]
                o_ref[r0:r0 + GMLP_WIN, lo:lo + GROUP_DIM] = o.astype(o_ref.dtype)


def _qkv_cols(h_ref, pos_ref, ropec_ref, w_ref, qkg_ref, qt_ref, k_ref, vt_ref, attn_scale):
    cos, sa, sb = _rope_tables(pos_ref, ropec_ref)
    bw = k_ref.shape[1]

    def normed(xh, gain):
        ms = jnp.mean(xh * xh, axis=-1, keepdims=True)
        return _rope_apply(xh * lax.rsqrt(ms + EPS) * gain, cos, sa, sb, B_ROT_DIM // 2)

    for c0 in range(0, w_ref.shape[1], PROJ_TN):
        acc = jnp.dot(h_ref[...], w_ref[:, c0:c0 + PROJ_TN], preferred_element_type=F32)
        for sub in range(PROJ_TN // GROUP_DIM):
            xh = acc[:, sub * GROUP_DIM:(sub + 1) * GROUP_DIM]
            col = c0 + sub * GROUP_DIM
            if col < bw:
                qh = normed(xh, qkg_ref[0:1, :]) * attn_scale
                qt_ref[col:col + GROUP_DIM, :] = jnp.transpose(qh).astype(qt_ref.dtype)
            elif col < 2 * bw:
                k_ref[:, col - bw:col - bw + GROUP_DIM] = normed(xh, qkg_ref[1:2, :]).astype(k_ref.dtype)
            else:
                vh = jnp.transpose(xh).astype(vt_ref.dtype)
                for kb in range(vt_ref.shape[0]):
                    vt_ref[kb, col - 2 * bw:col - 2 * bw + GROUP_DIM, :] = vh[:, kb * DSA_TILE:(kb + 1) * DSA_TILE]


def _idx_cols(h_ref, pos_ref, ropec_ref, w_ref, qit_ref, ki_ref, wt_ref, w_scale):
    cos, sa, sb = _rope_tables(pos_ref, ropec_ref)
    half = IDX_ROT_DIM // 2
    q_cols = IDX_HEADS * IDX_HEAD_DIM

    for c0 in range(0, w_ref.shape[1], PROJ_TN):
        tn = min(PROJ_TN, w_ref.shape[1] - c0)
        acc = jnp.dot(h_ref[...], w_ref[:, c0:c0 + tn], preferred_element_type=F32)
        if c0 < q_cols:
            for c in range(tn // LANES):
                xt = jnp.transpose(_rope_apply(acc[:, c * LANES:(c + 1) * LANES], cos, sa, sb, half))
                for sub in range(LANES // IDX_HEAD_DIM):
                    hd = (c0 + c * LANES) // IDX_HEAD_DIM + sub
                    qit_ref[hd] = xt[sub * IDX_HEAD_DIM:(sub + 1) * IDX_HEAD_DIM, :].astype(qit_ref.dtype)
        else:
            x = acc[:, 0:LANES]
            ki_ref[...] = _rope_apply(x, cos, sa, sb, half)[:, 0:IDX_HEAD_DIM].astype(ki_ref.dtype)
            wt_ref[...] = jnp.transpose(x * w_scale)[IDX_HEAD_DIM:IDX_HEAD_DIM + IDX_HEADS, :]


def _proj_kernel(x_ref, mod_ref, g_ref, pos_ref, ropeb_ref, ropei_ref,
                 w_ref, vg_ref, ws_ref, bs_ref, og_ref, qkg_ref,
                 ga_ref, qt_ref, k_ref, vt_ref, qit_ref, ki_ref, wt_ref, h_ref, *, row0, attn_scale, w_scale):
    x = x_ref[...]
    ms = jnp.mean(x * x, axis=-1, keepdims=True)
    h = x * lax.rsqrt(ms + EPS) * g_ref[...]
    h_ref[...] = (h * (1.0 + mod_ref[0, row0 + 1:row0 + 2, :]) + mod_ref[0, row0:row0 + 1, :]).astype(BF16)
    aw, bw = ga_ref.shape[1], k_ref.shape[1]
    wu_ref = w_ref.at[:, 0:aw]
    wv_ref = w_ref.at[:, aw:2 * aw]
    wqkv_ref = w_ref.at[:, 2 * aw:2 * aw + 3 * bw]
    widx_ref = w_ref.at[:, 2 * aw + 3 * bw:]
    _idx_cols(h_ref, pos_ref, ropei_ref, widx_ref, qit_ref, ki_ref, wt_ref, w_scale)
    _qkv_cols(h_ref, pos_ref, ropeb_ref, wqkv_ref, qkg_ref, qt_ref, k_ref, vt_ref, attn_scale)
    _gmlp_cols(h_ref, wu_ref, wv_ref, vg_ref, ws_ref, bs_ref, og_ref, ga_ref)


def _proj(x, mod, gain, pos, rope_b, rope_i, w, v_gain, ws, bs, out_gain, qk_gain, *, row0, seq, tm=512):
    t, d = x.shape
    aw = v_gain.shape[1]
    bw = d - aw
    bpb = seq // tm
    once = pl.Buffered(1)

    def resident(a):
        return pl.BlockSpec(a.shape, lambda i: (0,) * a.ndim, pipeline_mode=once)

    kern = functools.partial(_proj_kernel, row0=row0, attn_scale=float(GROUP_DIM) ** -0.5 * LOG2E,
                             w_scale=float(IDX_HEADS) ** -0.5 * float(IDX_HEAD_DIM) ** -0.5)
    return pl.pallas_call(
        kern,
        grid=(t // tm,),
        in_specs=[pl.BlockSpec((tm, d), lambda i: (i, 0)),
                  pl.BlockSpec((1,) + mod.shape[1:], lambda i: (i // bpb, 0, 0)),
                  pl.BlockSpec((1, d), lambda i: (0, 0)),
                  pl.BlockSpec((tm, 1), lambda i: (i, 0)),
                  resident(rope_b), resident(rope_i),
                  resident(w), resident(v_gain), resident(ws), resident(bs), resident(out_gain),
                  resident(qk_gain)],
        out_specs=[pl.BlockSpec((tm, aw), lambda i: (i, 0)),
                   pl.BlockSpec((bw, tm), lambda i: (0, i)),
                   pl.BlockSpec((tm, bw), lambda i: (i, 0)),
                   pl.BlockSpec((tm // DSA_TILE, bw, DSA_TILE), lambda i: (i, 0, 0)),
                   pl.BlockSpec((IDX_HEADS, IDX_HEAD_DIM, tm), lambda i: (0, 0, i)),
                   pl.BlockSpec((tm, IDX_HEAD_DIM), lambda i: (i, 0)),
                   pl.BlockSpec((IDX_HEADS, tm), lambda i: (0, i))],
        out_shape=[jax.ShapeDtypeStruct((t, aw), BF16),
                   jax.ShapeDtypeStruct((bw, t), BF16),
                   jax.ShapeDtypeStruct((t, bw), BF16),
                   jax.ShapeDtypeStruct((t // DSA_TILE, bw, DSA_TILE), BF16),
                   jax.ShapeDtypeStruct((IDX_HEADS, IDX_HEAD_DIM, t), BF16),
                   jax.ShapeDtypeStruct((t, IDX_HEAD_DIM), BF16),
                   jax.ShapeDtypeStruct((IDX_HEADS, t), F32)],
        scratch_shapes=[pltpu.VMEM((tm, d), BF16)],
        compiler_params=_cparams(("parallel",)),
        name="proj",
    )(x, mod, gain, pos, rope_b, rope_i, w, v_gain, ws, bs, out_gain, qk_gain)


def _dsa_kernel(qt_ref, k_ref, vt_ref, qit_ref, ki_ref, wt_ref, og_ref, o_ref,
                key_ref, hi_ref, lo_ref, bias_ref, sa_ref, sb_ref, m_ref, acc_ref, *, topk):
    _, tk, tq = key_ref.shape
    jq = pl.program_id(1)
    nt = jq + 1
    heads = qt_ref.shape[0] // GROUP_DIM
    q_chunk = (jq * tq + lax.broadcasted_iota(jnp.int32, (1, tq), 1)) // CHUNK

    def score_tile(kt, carry):
        k0 = pl.multiple_of(kt * tk, tk)
        for r0 in range(0, tk, SCORE_ROWS):
            ki = ki_ref[pl.ds(k0 + r0, SCORE_ROWS), :]
            acc = jnp.zeros((SCORE_ROWS, tq), F32)
            for hd in range(IDX_HEADS):
                logit = jnp.dot(ki, qit_ref[hd], preferred_element_type=F32)
                acc = acc + jnp.maximum(logit, 0.0) * wt_ref[hd:hd + 1, :]
            k_chunk = (k0 + r0 + lax.broadcasted_iota(jnp.int32, (SCORE_ROWS, 1), 0)) // CHUNK
            bits = pltpu.bitcast(jnp.where(k_chunk <= q_chunk, acc, NEG), jnp.int32)
            key = jnp.where(bits < 0, bits ^ 0x7FFFFFFF, bits)
            key_ref[kt, r0:r0 + SCORE_ROWS, :] = key
            hi_ref[kt, r0:r0 + SCORE_ROWS, :] = (key >> 16).astype(jnp.int16)
            lo_ref[kt, r0:r0 + SCORE_ROWS, :] = ((key & 0xFFFF) - 2 ** 15).astype(jnp.int16)
        return carry

    lax.fori_loop(0, nt, score_tile, 0)

    one16 = jnp.ones((I16_ROWS, tq), jnp.int16)
    zero16 = jnp.zeros((I16_ROWS, tq), jnp.int16)

    def count16(parts):
        return jnp.sum(sum(p.astype(jnp.int32) for p in parts), axis=0, keepdims=True)

    def select16(src_ref, need):
        def bit_step(i, u):
            cand_u = u | (jnp.int32(1) << (15 - i))
            cand = jnp.broadcast_to((cand_u - 2 ** 15).astype(jnp.int16), (I16_ROWS, tq))

            def count_tile(kt, parts):
                parts = list(parts)
                for r in range(tk // I16_ROWS):
                    ge = src_ref[kt, r * I16_ROWS:(r + 1) * I16_ROWS, :] >= cand
                    parts[r % COUNT_CHAINS] = parts[r % COUNT_CHAINS] + jnp.where(ge, one16, zero16)
                return tuple(parts)

            parts = lax.fori_loop(0, nt, count_tile, (zero16,) * COUNT_CHAINS)
            return jnp.where(count16(parts) >= need, cand_u, u)

        return lax.fori_loop(0, 16, bit_step, jnp.zeros((1, tq), jnp.int32))

    u_hi = select16(hi_ref, topk)
    t_hi = jnp.broadcast_to((u_hi - 2 ** 15).astype(jnp.int16), (I16_ROWS, tq))

    def split_tile(kt, parts):
        parts = list(parts)
        for r in range(tk // I16_ROWS):
            rows = slice(r * I16_ROWS, (r + 1) * I16_ROWS)
            hi = hi_ref[kt, rows, :]
            parts[r % COUNT_CHAINS] = parts[r % COUNT_CHAINS] + jnp.where(hi > t_hi, one16, zero16)
            lo_ref[kt, rows, :] = jnp.where(hi == t_hi, lo_ref[kt, rows, :], jnp.full_like(zero16, -(2 ** 15)))
        return tuple(parts)

    above = count16(lax.fori_loop(0, nt, split_tile, (zero16,) * COUNT_CHAINS))
    u_lo = select16(lo_ref, topk - above)
    thr = (u_hi - 2 ** 15) * 2 ** 16 + u_lo

    thr8 = jnp.broadcast_to(thr, (8, tq))

    def bias_tile(kt, parts):
        parts = list(parts)
        for r in range(tk // 8):
            ge = key_ref[kt, r * 8:(r + 1) * 8, :] >= thr8
            bias_ref[kt, r * 8:(r + 1) * 8, :] = jnp.where(ge, 0.0, NEG)
            parts[r % COUNT_CHAINS] = parts[r % COUNT_CHAINS] + jnp.where(ge, 1.0, 0.0)
        return tuple(parts)

    parts = lax.fori_loop(0, nt, bias_tile, (jnp.zeros((8, tq), F32),) * COUNT_CHAINS)
    excess = jnp.sum(sum(parts), axis=0, keepdims=True) - float(topk)

    @pl.when(jnp.max(excess) > 0.0)
    def _():
        def tied(kt):
            return key_ref[kt] == thr

        def index(kt):
            return kt * tk + lax.broadcasted_iota(jnp.int32, (tk, tq), 0)

        def count(pred_of_tile):
            def body(kt, acc):
                return acc + jnp.sum(jnp.where(pred_of_tile(kt), 1.0, 0.0), axis=0, keepdims=True)
            return lax.fori_loop(0, nt, body, jnp.zeros((1, tq), F32))

        keep = count(tied) - excess
        nbits = (tk * key_ref.shape[0] - 1).bit_length()

        def bit_step(i, last):
            cand = last | (jnp.int32(1) << (nbits - 1 - i))
            below = count(lambda kt: tied(kt) & (index(kt) < cand))
            return jnp.where(below < keep, cand, last)

        last = lax.fori_loop(0, nbits, bit_step, jnp.zeros((1, tq), jnp.int32))

        def drop_tile(kt, carry):
            bias_ref[kt] = jnp.where(tied(kt) & (index(kt) > last), NEG, bias_ref[kt])
            return carry

        lax.fori_loop(0, nt, drop_tile, 0)

    k_chunk = (jq * tk + lax.broadcasted_iota(jnp.int32, (tk, tq), 0)) // CHUNK
    bias_ref[jq] = jnp.where(k_chunk <= q_chunk, bias_ref[jq], NEG)

    m_ref[...] = jnp.full(m_ref.shape, NEG, F32)
    acc_ref[...] = jnp.zeros(acc_ref.shape, F32)
    ones_rows = jnp.ones((DENOM_ROWS, tk), BF16)

    def score_head(kt, s_ref, hd):
        k0 = pl.multiple_of(kt * tk, tk)
        lo = hd * GROUP_DIM
        s_ref[hd] = jnp.dot(k_ref[pl.ds(k0, tk), lo:lo + GROUP_DIM], qt_ref[lo:lo + GROUP_DIM, :],
                            preferred_element_type=F32)

    def softmax_pv_head(kt, s_ref, hd):
        lo = hd * GROUP_DIM
        ps, alphas = [], []
        for c0 in range(0, tq, LANES):
            s = s_ref[hd, :, c0:c0 + LANES] + bias_ref[kt, :, c0:c0 + LANES]
            m_prev = m_ref[hd, :, c0:c0 + LANES]
            m_new = jnp.maximum(m_prev, jnp.max(s, axis=0, keepdims=True))
            m_ref[hd, :, c0:c0 + LANES] = m_new
            alphas.append(jnp.exp2(m_prev - m_new))
            ps.append(jnp.exp2(s - m_new).astype(BF16))
        alpha = jnp.concatenate(alphas, axis=1)
        vt1 = jnp.concatenate([vt_ref[kt, lo:lo + GROUP_DIM, :], ones_rows], axis=0)
        acc_ref[hd] = alpha * acc_ref[hd] + jnp.dot(vt1, jnp.concatenate(ps, axis=1), preferred_element_type=F32)

    def scores_into(kt, s_ref):
        for hd in range(heads):
            score_head(kt, s_ref, hd)

    def softmax_pv(kt, s_ref, next_kt, next_ref):
        for hd in range(heads):
            score_head(next_kt, next_ref, hd)
            softmax_pv_head(kt, s_ref, hd)

    @pl.when(nt % 2 == 1)
    def _():
        bias_ref[nt] = jnp.full((tk, tq), NEG, F32)

    scores_into(0, sa_ref)

    def attn_pair(j, carry):
        t0 = 2 * j
        softmax_pv(t0, sa_ref, t0 + 1, sb_ref)
        softmax_pv(t0 + 1, sb_ref, jnp.minimum(t0 + 2, nt - 1), sa_ref)
        return carry

    lax.fori_loop(0, (nt + 1) // 2, attn_pair, 0)

    for hd in range(heads):
        lo = hd * GROUP_DIM
        o = jnp.transpose(acc_ref[hd, 0:GROUP_DIM, :] / acc_ref[hd, GROUP_DIM:GROUP_DIM + 1, :])
        ms = jnp.mean(o * o, axis=-1, keepdims=True)
        o = o * lax.rsqrt(ms + EPS) * og_ref[:, lo:lo + GROUP_DIM]
        o_ref[:, lo:lo + GROUP_DIM] = o.astype(o_ref.dtype)


def _dsa(qt, k, vt, qit, ki, wt, out_gain, *, bsz, seq):
    bw, t = qt.shape
    heads = bw // GROUP_DIM
    tq = tk = DSA_TILE
    nq = seq // tq
    topk = min(TOPK_MAX, seq // 4)
    assert tq >= topk and tq % CHUNK == 0 and nq % 2 == 0
    once = pl.Buffered(1)
    return pl.pallas_call(
        functools.partial(_dsa_kernel, topk=topk),
        grid=(bsz, nq),
        in_specs=[pl.BlockSpec((bw, tq), lambda b, j: (0, b * nq + j)),
                  pl.BlockSpec((seq, bw), lambda b, j: (b, 0), pipeline_mode=once),
                  pl.BlockSpec((nq, bw, tk), lambda b, j: (b, 0, 0), pipeline_mode=once),
                  pl.BlockSpec((IDX_HEADS, IDX_HEAD_DIM, tq), lambda b, j: (0, 0, b * nq + j)),
                  pl.BlockSpec((seq, IDX_HEAD_DIM), lambda b, j: (b, 0), pipeline_mode=once),
                  pl.BlockSpec((IDX_HEADS, tq), lambda b, j: (0, b * nq + j)),
                  pl.BlockSpec((1, bw), lambda b, j: (0, 0))],
        out_specs=pl.BlockSpec((tq, bw), lambda b, j: (b * nq + j, 0)),
        out_shape=jax.ShapeDtypeStruct((t, bw), BF16),
        scratch_shapes=[pltpu.VMEM((nq, tk, tq), jnp.int32),
                        pltpu.VMEM((nq, tk, tq), jnp.int16),
                        pltpu.VMEM((nq, tk, tq), jnp.int16),
                        pltpu.VMEM((nq, tk, tq), F32),
                        pltpu.VMEM((heads, tk, tq), F32),
                        pltpu.VMEM((heads, tk, tq), F32),
                        pltpu.VMEM((heads, 1, tq), F32),
                        pltpu.VMEM((heads, GROUP_DIM + DENOM_ROWS, tq), F32)],
        compiler_params=_cparams(("parallel", "arbitrary")),
        name="dsa",
    )(qt, k, vt, qit, ki, wt, out_gain)


def _rope_consts(rot_dim, period):
    half = rot_dim // 2
    inv = np.float32(ROPE_THETA) ** (np.float32(-2.0) * np.arange(half, dtype=np.float32) / np.float32(rot_dim))
    lane = np.arange(LANES) % period
    table = np.zeros((8, LANES), np.float32)
    table[0] = np.where(lane < rot_dim, inv[lane % half], 0.0)
    table[1] = np.where((lane >= half) & (lane < rot_dim), 1.0, 0.0)
    table[2] = np.where(lane < half, -1.0, 0.0)
    return jnp.asarray(table)


def kernel(x, c, positions, w_ada, b_ada, norm1_g, ffn1_w1, ffn1_w3, ffn1_w2, norm2_g, w_in, gmlp_v_g, gmlp_ws,
           gmlp_b, q_norm_g, k_norm_g, out_norm_g, w_out, norm3_g, ffn2_w1, ffn2_w3, ffn2_w2):
    bsz, seq, d = x.shape
    depth = w_ada.shape[0]
    aw = gmlp_v_g.shape[1]
    bw = d - aw
    t = bsz * seq

    xf = x.reshape(t, d)
    pos = positions.reshape(t, 1)
    c_pad = jnp.zeros((8, d), F32).at[:bsz].set(c)
    rope_b = _rope_consts(B_ROT_DIM, GROUP_DIM)
    rope_i = _rope_consts(IDX_ROT_DIM, IDX_HEAD_DIM)
    idx_cols = IDX_HEADS * IDX_HEAD_DIM + IDX_HEAD_DIM + IDX_HEADS
    idx_pad = -idx_cols % 256

    for l in range(depth):
        mod = _ada(c_pad, w_ada[l], b_ada[l][None, :])[:bsz].reshape(bsz, 9, d)

        xf = _ffn(xf, mod, norm1_g[l][None, :], ffn1_w1[l].astype(BF16), ffn1_w3[l].astype(BF16),
                  ffn1_w2[l].astype(BF16), row0=0, seq=seq)

        w_all = jnp.pad(w_in[l], ((0, 0), (0, idx_pad))).astype(BF16)
        g2 = norm2_g[l][None, :]
        bs = jnp.broadcast_to(gmlp_b[l][:, :, None], gmlp_b.shape[1:] + (GROUP_DIM,))

        ga, qt, k, vt, qit, ki, wt = _proj(
            xf, mod, g2, pos, rope_b, rope_i, w_all, gmlp_v_g[l][None, :], gmlp_ws[l], bs,
            out_norm_g[l][None, :aw], jnp.stack([q_norm_g[l], k_norm_g[l]]), row0=3, seq=seq)
        gb = _dsa(qt, k, vt, qit, ki, wt, out_norm_g[l][None, aw:], bsz=bsz, seq=seq)
        xf = _out(xf, mod, ga, gb, w_out[l].astype(BF16), row0=5, seq=seq)

        xf = _ffn(xf, mod, norm3_g[l][None, :], ffn2_w1[l].astype(BF16), ffn2_w3[l].astype(BF16),
                  ffn2_w2[l].astype(BF16), row0=6, seq=seq)
    return xf.reshape(bsz, seq, d)
```

```python
import functools

import jax
import jax.numpy as jnp
import numpy as np
from jax import lax
from jax.experimental import pallas as pl
from jax.experimental.pallas import tpu as pltpu

F32 = jnp.float32
BF16 = jnp.bfloat16

CHUNK = 64
GMLP_WIN = 128
GROUP_DIM = 128
IDX_HEADS = 16
IDX_HEAD_DIM = 64
TOPK_MAX = 256
ROPE_THETA = 500000.0
B_ROT_DIM = GROUP_DIM // 4
IDX_ROT_DIM = IDX_HEAD_DIM // 4
EPS = 1e-6
NEG = -1e30

LANES = 128
I16_ROWS = 16
DENOM_ROWS = 16
LOG2E = 1.4426950408889634
NORM_ROWS = 16
NORM_UNROLL = 8
FFN_SUB = 256
PROJ_TN = 512
DSA_TILE = 256
SCORE_ROWS = 64
COUNT_CHAINS = 4
VMEM_LIMIT = 62 * 1024 * 1024


def _cparams(sem):
    return pltpu.CompilerParams(dimension_semantics=sem, vmem_limit_bytes=VMEM_LIMIT)


def _rms_mod_rows(x_ref, h_ref, acc_ref, g_ref, mod_ref, row0):
    gain = g_ref[...] * (1.0 + mod_ref[0, row0 + 1:row0 + 2, :])
    shift = mod_ref[0, row0:row0 + 1, :]

    def body(c, carry):
        r0 = pl.multiple_of(c * NORM_ROWS, NORM_ROWS)
        x = x_ref[pl.ds(r0, NORM_ROWS), :]
        ms = jnp.mean(x * x, axis=-1, keepdims=True)
        h_ref[pl.ds(r0, NORM_ROWS), :] = (x * lax.rsqrt(ms + EPS) * gain + shift).astype(h_ref.dtype)
        acc_ref[pl.ds(r0, NORM_ROWS), :] = jnp.zeros((NORM_ROWS, acc_ref.shape[1]), acc_ref.dtype)
        return carry

    lax.fori_loop(0, x_ref.shape[0] // NORM_ROWS, body, 0, unroll=NORM_UNROLL)


def _rope_tables(pos_ref, ropec_ref):
    ang = pos_ref[...].astype(F32) * ropec_ref[0:1, :]
    cos = jnp.cos(ang)
    sin = jnp.sin(ang)
    return cos, sin * ropec_ref[1:2, :], sin * ropec_ref[2:3, :]


def _rope_apply(x, cos, sa, sb, half):
    return x * cos + pltpu.roll(x, half, 1) * sa + pltpu.roll(x, LANES - half, 1) * sb


def _ada_kernel(c_ref, w_ref, b_ref, o_ref):
    c = c_ref[...]
    act = c * jax.nn.sigmoid(c)
    o_ref[...] = jnp.dot(act, w_ref[...], preferred_element_type=F32) + b_ref[...]


def _ada(c_pad, w_ada, b_ada, tn=1024):
    rows, d = c_pad.shape
    n = w_ada.shape[1]
    return pl.pallas_call(
        _ada_kernel,
        grid=(n // tn,),
        in_specs=[pl.BlockSpec((rows, d), lambda j: (0, 0)),
                  pl.BlockSpec((d, tn), lambda j: (0, j)),
                  pl.BlockSpec((1, tn), lambda j: (0, j))],
        out_specs=pl.BlockSpec((rows, tn), lambda j: (0, j)),
        out_shape=jax.ShapeDtypeStruct((rows, n), F32),
        compiler_params=_cparams(("arbitrary",)),
        name="ada",
    )(c_pad, w_ada, b_ada)


def _ffn_kernel(x_ref, mod_ref, g_ref, w1_ref, w3_ref, w2_ref, o_ref, h_ref, *, row0):
    j = pl.program_id(1)

    @pl.when(j == 0)
    def _():
        _rms_mod_rows(x_ref, h_ref, o_ref, g_ref, mod_ref, row0)

    h = h_ref[...]
    tf = w1_ref.shape[1]
    ab = [(jnp.dot(h, w1_ref[:, c0:c0 + FFN_SUB], preferred_element_type=F32),
           jnp.dot(h, w3_ref[:, c0:c0 + FFN_SUB], preferred_element_type=F32)) for c0 in range(0, tf, FFN_SUB)]
    upd = None
    for idx, (a, b) in enumerate(ab):
        act = (a * jax.nn.sigmoid(a) * b).astype(BF16)
        part = jnp.dot(act, w2_ref[idx * FFN_SUB:(idx + 1) * FFN_SUB, :], preferred_element_type=F32)
        upd = part if upd is None else upd + part
    o_ref[...] += upd

    @pl.when(j == pl.num_programs(1) - 1)
    def _():
        o_ref[...] = x_ref[...] + (0.5 * mod_ref[0, row0 + 2:row0 + 3, :]) * o_ref[...]


def _ffn(x, mod, gain, w1, w3, w2, *, row0, seq, tm=1024, tf=512):
    t, d = x.shape
    f = w1.shape[1]
    bpb = seq // tm
    return pl.pallas_call(
        functools.partial(_ffn_kernel, row0=row0),
        grid=(t // tm, f // tf),
        in_specs=[pl.BlockSpec((tm, d), lambda i, j: (i, 0)),
                  pl.BlockSpec((1,) + mod.shape[1:], lambda i, j: (i // bpb, 0, 0)),
                  pl.BlockSpec((1, d), lambda i, j: (0, 0)),
                  pl.BlockSpec((d, tf), lambda i, j: (0, j)),
                  pl.BlockSpec((d, tf), lambda i, j: (0, j)),
                  pl.BlockSpec((tf, d), lambda i, j: (j, 0))],
        out_specs=pl.BlockSpec((tm, d), lambda i, j: (i, 0)),
        out_shape=jax.ShapeDtypeStruct((t, d), F32),
        scratch_shapes=[pltpu.VMEM((tm, d), BF16)],
        compiler_params=_cparams(("parallel", "arbitrary")),
        name="ffn",
    )(x, mod, gain, w1, w3, w2)


def _out_kernel(x_ref, mod_ref, a_ref, b_ref, wo_ref, o_ref, *, row0):
    half = a_ref.shape[1]
    mixed = jnp.dot(a_ref[...], wo_ref[0:half, :], preferred_element_type=F32)
    mixed = mixed + jnp.dot(b_ref[...], wo_ref[half:, :], preferred_element_type=F32)
    o_ref[...] = x_ref[...] + mod_ref[0, row0:row0 + 1, :] * mixed


def _out(x, mod, ga, gb, w_out, *, row0, seq, tm=512):
    t, d = x.shape
    bpb = seq // tm
    return pl.pallas_call(
        functools.partial(_out_kernel, row0=row0),
        grid=(t // tm,),
        in_specs=[pl.BlockSpec((tm, d), lambda i: (i, 0)),
                  pl.BlockSpec((1,) + mod.shape[1:], lambda i: (i // bpb, 0, 0)),
                  pl.BlockSpec((tm, ga.shape[1]), lambda i: (i, 0)),
                  pl.BlockSpec((tm, gb.shape[1]), lambda i: (i, 0)),
                  pl.BlockSpec(w_out.shape, lambda i: (0, 0), pipeline_mode=pl.Buffered(1))],
        out_specs=pl.BlockSpec((tm, d), lambda i: (i, 0)),
        out_shape=jax.ShapeDtypeStruct((t, d), F32),
        compiler_params=_cparams(("parallel",)),
        name="out",
    )(x, mod, ga, gb, w_out)


def _gmlp_cols(h_ref, wu_ref, wv_ref, vg_ref, ws_ref, bs_ref, og_ref, o_ref):
    tm = h_ref.shape[0]
    ri = lax.broadcasted_iota(jnp.int32, (GMLP_WIN, GMLP_WIN), 0) // CHUNK
    ci = lax.broadcasted_iota(jnp.int32, (GMLP_WIN, GMLP_WIN), 1) // CHUNK
    for c0 in range(0, wu_ref.shape[1], PROJ_TN):
        u = jax.nn.gelu(jnp.dot(h_ref[...], wu_ref[:, c0:c0 + PROJ_TN], preferred_element_type=F32))
        v = jax.nn.gelu(jnp.dot(h_ref[...], wv_ref[:, c0:c0 + PROJ_TN], preferred_element_type=F32))
        for sub in range(PROJ_TN // GROUP_DIM):
            lo = c0 + sub * GROUP_DIM
            g = lo // GROUP_DIM
            vg = v[:, sub * GROUP_DIM:(sub + 1) * GROUP_DIM]
            mu = jnp.mean(vg, axis=-1, keepdims=True)
            vc = vg - mu
            var = jnp.mean(vc * vc, axis=-1, keepdims=True)
            y = (vc * lax.rsqrt(var + EPS) * vg_ref[:, lo:lo + GROUP_DIM]).astype(BF16)
            w = jnp.where(ci <= ri, ws_ref[g], 0.0).astype(BF16)
            for win in range(tm // GMLP_WIN):
                r0 = win * GMLP_WIN
                mixed = jnp.dot(w, y[r0:r0 + GMLP_WIN], preferred_element_type=F32) + bs_ref[g]
                o = u[r0:r0 + GMLP_WIN, sub * GROUP_DIM:(sub + 1) * GROUP_DIM] * mixed
                ms = jnp.mean(o * o, axis=-1, keepdims=True)
                o = o * lax.rsqrt(ms + EPS) * og_ref[:, lo:lo + GROUP_DIM]
                o_ref[r0:r0 + GMLP_WIN, lo:lo + GROUP_DIM] = o.astype(o_ref.dtype)


def _qkv_cols(h_ref, pos_ref, ropec_ref, w_ref, qkg_ref, qt_ref, k_ref, vt_ref, attn_scale):
    cos, sa, sb = _rope_tables(pos_ref, ropec_ref)
    bw = k_ref.shape[1]

    def normed(xh, gain):
        ms = jnp.mean(xh * xh, axis=-1, keepdims=True)
        return _rope_apply(xh * lax.rsqrt(ms + EPS) * gain, cos, sa, sb, B_ROT_DIM // 2)

    for c0 in range(0, w_ref.shape[1], PROJ_TN):
        acc = jnp.dot(h_ref[...], w_ref[:, c0:c0 + PROJ_TN], preferred_element_type=F32)
        for sub in range(PROJ_TN // GROUP_DIM):
            xh = acc[:, sub * GROUP_DIM:(sub + 1) * GROUP_DIM]
            col = c0 + sub * GROUP_DIM
            if col < bw:
                qh = normed(xh, qkg_ref[0:1, :]) * attn_scale
                qt_ref[col:col + GROUP_DIM, :] = jnp.transpose(qh).astype(qt_ref.dtype)
            elif col < 2 * bw:
                k_ref[:, col - bw:col - bw + GROUP_DIM] = normed(xh, qkg_ref[1:2, :]).astype(k_ref.dtype)
            else:
                vh = jnp.transpose(xh).astype(vt_ref.dtype)
                for kb in range(vt_ref.shape[0]):
                    vt_ref[kb, col - 2 * bw:col - 2 * bw + GROUP_DIM, :] = vh[:, kb * DSA_TILE:(kb + 1) * DSA_TILE]


def _idx_cols(h_ref, pos_ref, ropec_ref, w_ref, qit_ref, ki_ref, wt_ref, w_scale):
    cos, sa, sb = _rope_tables(pos_ref, ropec_ref)
    half = IDX_ROT_DIM // 2
    q_cols = IDX_HEADS * IDX_HEAD_DIM

    for c0 in range(0, w_ref.shape[1], PROJ_TN):
        tn = min(PROJ_TN, w_ref.shape[1] - c0)
        acc = jnp.dot(h_ref[...], w_ref[:, c0:c0 + tn], preferred_element_type=F32)
        if c0 < q_cols:
            for c in range(tn // LANES):
                xt = jnp.transpose(_rope_apply(acc[:, c * LANES:(c + 1) * LANES], cos, sa, sb, half))
                for sub in range(LANES // IDX_HEAD_DIM):
                    hd = (c0 + c * LANES) // IDX_HEAD_DIM + sub
                    qit_ref[hd] = xt[sub * IDX_HEAD_DIM:(sub + 1) * IDX_HEAD_DIM, :].astype(qit_ref.dtype)
        else:
            x = acc[:, 0:LANES]
            ki_ref[...] = _rope_apply(x, cos, sa, sb, half)[:, 0:IDX_HEAD_DIM].astype(ki_ref.dtype)
            wt_ref[...] = jnp.transpose(x * w_scale)[IDX_HEAD_DIM:IDX_HEAD_DIM + IDX_HEADS, :]


def _proj_kernel(x_ref, mod_ref, g_ref, pos_ref, ropeb_ref, ropei_ref,
                 w_ref, vg_ref, ws_ref, bs_ref, og_ref, qkg_ref,
                 ga_ref, qt_ref, k_ref, vt_ref, qit_ref, ki_ref, wt_ref, h_ref, *, row0, attn_scale, w_scale):
    x = x_ref[...]
    ms = jnp.mean(x * x, axis=-1, keepdims=True)
    h = x * lax.rsqrt(ms + EPS) * g_ref[...]
    h_ref[...] = (h * (1.0 + mod_ref[0, row0 + 1:row0 + 2, :]) + mod_ref[0, row0:row0 + 1, :]).astype(BF16)
    aw, bw = ga_ref.shape[1], k_ref.shape[1]
    wu_ref = w_ref.at[:, 0:aw]
    wv_ref = w_ref.at[:, aw:2 * aw]
    wqkv_ref = w_ref.at[:, 2 * aw:2 * aw + 3 * bw]
    widx_ref = w_ref.at[:, 2 * aw + 3 * bw:]
    _idx_cols(h_ref, pos_ref, ropei_ref, widx_ref, qit_ref, ki_ref, wt_ref, w_scale)
    _qkv_cols(h_ref, pos_ref, ropeb_ref, wqkv_ref, qkg_ref, qt_ref, k_ref, vt_ref, attn_scale)
    _gmlp_cols(h_ref, wu_ref, wv_ref, vg_ref, ws_ref, bs_ref, og_ref, ga_ref)


def _proj(x, mod, gain, pos, rope_b, rope_i, w, v_gain, ws, bs, out_gain, qk_gain, *, row0, seq, tm=512):
    t, d = x.shape
    aw = v_gain.shape[1]
    bw = d - aw
    bpb = seq // tm
    once = pl.Buffered(1)

    def resident(a):
        return pl.BlockSpec(a.shape, lambda i: (0,) * a.ndim, pipeline_mode=once)

    kern = functools.partial(_proj_kernel, row0=row0, attn_scale=float(GROUP_DIM) ** -0.5 * LOG2E,
                             w_scale=float(IDX_HEADS) ** -0.5 * float(IDX_HEAD_DIM) ** -0.5)
    return pl.pallas_call(
        kern,
        grid=(t // tm,),
        in_specs=[pl.BlockSpec((tm, d), lambda i: (i, 0)),
                  pl.BlockSpec((1,) + mod.shape[1:], lambda i: (i // bpb, 0, 0)),
                  pl.BlockSpec((1, d), lambda i: (0, 0)),
                  pl.BlockSpec((tm, 1), lambda i: (i, 0)),
                  resident(rope_b), resident(rope_i),
                  resident(w), resident(v_gain), resident(ws), resident(bs), resident(out_gain),
                  resident(qk_gain)],
        out_specs=[pl.BlockSpec((tm, aw), lambda i: (i, 0)),
                   pl.BlockSpec((bw, tm), lambda i: (0, i)),
                   pl.BlockSpec((tm, bw), lambda i: (i, 0)),
                   pl.BlockSpec((tm // DSA_TILE, bw, DSA_TILE), lambda i: (i, 0, 0)),
                   pl.BlockSpec((IDX_HEADS, IDX_HEAD_DIM, tm), lambda i: (0, 0, i)),
                   pl.BlockSpec((tm, IDX_HEAD_DIM), lambda i: (i, 0)),
                   pl.BlockSpec((IDX_HEADS, tm), lambda i: (0, i))],
        out_shape=[jax.ShapeDtypeStruct((t, aw), BF16),
                   jax.ShapeDtypeStruct((bw, t), BF16),
                   jax.ShapeDtypeStruct((t, bw), BF16),
                   jax.ShapeDtypeStruct((t // DSA_TILE, bw, DSA_TILE), BF16),
                   jax.ShapeDtypeStruct((IDX_HEADS, IDX_HEAD_DIM, t), BF16),
                   jax.ShapeDtypeStruct((t, IDX_HEAD_DIM), BF16),
                   jax.ShapeDtypeStruct((IDX_HEADS, t), F32)],
        scratch_shapes=[pltpu.VMEM((tm, d), BF16)],
        compiler_params=_cparams(("parallel",)),
        name="proj",
    )(x, mod, gain, pos, rope_b, rope_i, w, v_gain, ws, bs, out_gain, qk_gain)


def _dsa_kernel(qt_ref, k_ref, vt_ref, qit_ref, ki_ref, wt_ref, og_ref, o_ref,
                key_ref, hi_ref, lo_ref, bias_ref, sa_ref, sb_ref, m_ref, acc_ref, *, topk):
    _, tk, tq = key_ref.shape
    jq = pl.program_id(1)
    nt = jq + 1
    heads = qt_ref.shape[0] // GROUP_DIM
    q_chunk = (jq * tq + lax.broadcasted_iota(jnp.int32, (1, tq), 1)) // CHUNK

    def score_tile(kt, carry):
        k0 = pl.multiple_of(kt * tk, tk)
        for r0 in range(0, tk, SCORE_ROWS):
            ki = ki_ref[pl.ds(k0 + r0, SCORE_ROWS), :]
            acc = jnp.zeros((SCORE_ROWS, tq), F32)
            for hd in range(IDX_HEADS):
                logit = jnp.dot(ki, qit_ref[hd], preferred_element_type=F32)
                acc = acc + jnp.maximum(logit, 0.0) * wt_ref[hd:hd + 1, :]
            k_chunk = (k0 + r0 + lax.broadcasted_iota(jnp.int32, (SCORE_ROWS, 1), 0)) // CHUNK
            bits = pltpu.bitcast(jnp.where(k_chunk <= q_chunk, acc, NEG), jnp.int32)
            key = jnp.where(bits < 0, bits ^ 0x7FFFFFFF, bits)
            key_ref[kt, r0:r0 + SCORE_ROWS, :] = key
            hi_ref[kt, r0:r0 + SCORE_ROWS, :] = (key >> 16).astype(jnp.int16)
            lo_ref[kt, r0:r0 + SCORE_ROWS, :] = ((key & 0xFFFF) - 2 ** 15).astype(jnp.int16)
        return carry

    lax.fori_loop(0, nt, score_tile, 0)

    one16 = jnp.ones((I16_ROWS, tq), jnp.int16)
    zero16 = jnp.zeros((I16_ROWS, tq), jnp.int16)

    def count16(parts):
        return jnp.sum(sum(p.astype(jnp.int32) for p in parts), axis=0, keepdims=True)

    def select16(src_ref, need):
        def bit_step(i, u):
            cand_u = u | (jnp.int32(1) << (15 - i))
            cand = jnp.broadcast_to((cand_u - 2 ** 15).astype(jnp.int16), (I16_ROWS, tq))

            def count_tile(kt, parts):
                parts = list(parts)
                for r in range(tk // I16_ROWS):
                    ge = src_ref[kt, r * I16_ROWS:(r + 1) * I16_ROWS, :] >= cand
                    parts[r % COUNT_CHAINS] = parts[r % COUNT_CHAINS] + jnp.where(ge, one16, zero16)
                return tuple(parts)

            parts = lax.fori_loop(0, nt, count_tile, (zero16,) * COUNT_CHAINS)
            return jnp.where(count16(parts) >= need, cand_u, u)

        return lax.fori_loop(0, 16, bit_step, jnp.zeros((1, tq), jnp.int32))

    u_hi = select16(hi_ref, topk)
    t_hi = jnp.broadcast_to((u_hi - 2 ** 15).astype(jnp.int16), (I16_ROWS, tq))

    def split_tile(kt, parts):
        parts = list(parts)
        for r in range(tk // I16_ROWS):
            rows = slice(r * I16_ROWS, (r + 1) * I16_ROWS)
            hi = hi_ref[kt, rows, :]
            parts[r % COUNT_CHAINS] = parts[r % COUNT_CHAINS] + jnp.where(hi > t_hi, one16, zero16)
            lo_ref[kt, rows, :] = jnp.where(hi == t_hi, lo_ref[kt, rows, :], jnp.full_like(zero16, -(2 ** 15)))
        return tuple(parts)

    above = count16(lax.fori_loop(0, nt, split_tile, (zero16,) * COUNT_CHAINS))
    u_lo = select16(lo_ref, topk - above)
    thr = (u_hi - 2 ** 15) * 2 ** 16 + u_lo

    thr8 = jnp.broadcast_to(thr, (8, tq))

    def bias_tile(kt, parts):
        parts = list(parts)
        for r in range(tk // 8):
            ge = key_ref[kt, r * 8:(r + 1) * 8, :] >= thr8
            bias_ref[kt, r * 8:(r + 1) * 8, :] = jnp.where(ge, 0.0, NEG)
            parts[r % COUNT_CHAINS] = parts[r % COUNT_CHAINS] + jnp.where(ge, 1.0, 0.0)
        return tuple(parts)

    parts = lax.fori_loop(0, nt, bias_tile, (jnp.zeros((8, tq), F32),) * COUNT_CHAINS)
    excess = jnp.sum(sum(parts), axis=0, keepdims=True) - float(topk)

    @pl.when(jnp.max(excess) > 0.0)
    def _():
        def tied(kt):
            return key_ref[kt] == thr

        def index(kt):
            return kt * tk + lax.broadcasted_iota(jnp.int32, (tk, tq), 0)

        def count(pred_of_tile):
            def body(kt, acc):
                return acc + jnp.sum(jnp.where(pred_of_tile(kt), 1.0, 0.0), axis=0, keepdims=True)
            return lax.fori_loop(0, nt, body, jnp.zeros((1, tq), F32))

        keep = count(tied) - excess
        nbits = (tk * key_ref.shape[0] - 1).bit_length()

        def bit_step(i, last):
            cand = last | (jnp.int32(1) << (nbits - 1 - i))
            below = count(lambda kt: tied(kt) & (index(kt) < cand))
            return jnp.where(below < keep, cand, last)

        last = lax.fori_loop(0, nbits, bit_step, jnp.zeros((1, tq), jnp.int32))

        def drop_tile(kt, carry):
            bias_ref[kt] = jnp.where(tied(kt) & (index(kt) > last), NEG, bias_ref[kt])
            return carry

        lax.fori_loop(0, nt, drop_tile, 0)

    k_chunk = (jq * tk + lax.broadcasted_iota(jnp.int32, (tk, tq), 0)) // CHUNK
    bias_ref[jq] = jnp.where(k_chunk <= q_chunk, bias_ref[jq], NEG)

    m_ref[...] = jnp.full(m_ref.shape, NEG, F32)
    acc_ref[...] = jnp.zeros(acc_ref.shape, F32)
    ones_rows = jnp.ones((DENOM_ROWS, tk), BF16)

    def score_head(kt, s_ref, hd):
        k0 = pl.multiple_of(kt * tk, tk)
        lo = hd * GROUP_DIM
        s_ref[hd] = jnp.dot(k_ref[pl.ds(k0, tk), lo:lo + GROUP_DIM], qt_ref[lo:lo + GROUP_DIM, :],
                            preferred_element_type=F32)

    def softmax_pv_head(kt, s_ref, hd):
        lo = hd * GROUP_DIM
        ps, alphas = [], []
        for c0 in range(0, tq, LANES):
            s = s_ref[hd, :, c0:c0 + LANES] + bias_ref[kt, :, c0:c0 + LANES]
            m_prev = m_ref[hd, :, c0:c0 + LANES]
            m_new = jnp.maximum(m_prev, jnp.max(s, axis=0, keepdims=True))
            m_ref[hd, :, c0:c0 + LANES] = m_new
            alphas.append(jnp.exp2(m_prev - m_new))
            ps.append(jnp.exp2(s - m_new).astype(BF16))
        alpha = jnp.concatenate(alphas, axis=1)
        vt1 = jnp.concatenate([vt_ref[kt, lo:lo + GROUP_DIM, :], ones_rows], axis=0)
        acc_ref[hd] = alpha * acc_ref[hd] + jnp.dot(vt1, jnp.concatenate(ps, axis=1), preferred_element_type=F32)

    def scores_into(kt, s_ref):
        for hd in range(heads):
            score_head(kt, s_ref, hd)

    def softmax_pv(kt, s_ref, next_kt, next_ref):
        for hd in range(heads):
            score_head(next_kt, next_ref, hd)
            softmax_pv_head(kt, s_ref, hd)

    @pl.when(nt % 2 == 1)
    def _():
        bias_ref[nt] = jnp.full((tk, tq), NEG, F32)

    scores_into(0, sa_ref)

    def attn_pair(j, carry):
        t0 = 2 * j
        softmax_pv(t0, sa_ref, t0 + 1, sb_ref)
        softmax_pv(t0 + 1, sb_ref, jnp.minimum(t0 + 2, nt - 1), sa_ref)
        return carry

    lax.fori_loop(0, (nt + 1) // 2, attn_pair, 0)

    for hd in range(heads):
        lo = hd * GROUP_DIM
        o = acc_ref[hd, 0:GROUP_DIM, :] / acc_ref[hd, GROUP_DIM:GROUP_DIM + 1, :]
        ms = jnp.mean(o * o, axis=0, keepdims=True)
        o = jnp.transpose(o * lax.rsqrt(ms + EPS)) * og_ref[:, lo:lo + GROUP_DIM]
        o_ref[:, lo:lo + GROUP_DIM] = o.astype(o_ref.dtype)


def _dsa(qt, k, vt, qit, ki, wt, out_gain, *, bsz, seq):
    bw, t = qt.shape
    heads = bw // GROUP_DIM
    tq = tk = DSA_TILE
    nq = seq // tq
    topk = min(TOPK_MAX, seq // 4)
    assert tq >= topk and tq % CHUNK == 0 and nq % 2 == 0
    once = pl.Buffered(1)
    return pl.pallas_call(
        functools.partial(_dsa_kernel, topk=topk),
        grid=(bsz, nq),
        in_specs=[pl.BlockSpec((bw, tq), lambda b, j: (0, b * nq + j)),
                  pl.BlockSpec((seq, bw), lambda b, j: (b, 0), pipeline_mode=once),
                  pl.BlockSpec((nq, bw, tk), lambda b, j: (b, 0, 0), pipeline_mode=once),
                  pl.BlockSpec((IDX_HEADS, IDX_HEAD_DIM, tq), lambda b, j: (0, 0, b * nq + j)),
                  pl.BlockSpec((seq, IDX_HEAD_DIM), lambda b, j: (b, 0), pipeline_mode=once),
                  pl.BlockSpec((IDX_HEADS, tq), lambda b, j: (0, b * nq + j)),
                  pl.BlockSpec((1, bw), lambda b, j: (0, 0))],
        out_specs=pl.BlockSpec((tq, bw), lambda b, j: (b * nq + j, 0)),
        out_shape=jax.ShapeDtypeStruct((t, bw), BF16),
        scratch_shapes=[pltpu.VMEM((nq, tk, tq), jnp.int32),
                        pltpu.VMEM((nq, tk, tq), jnp.int16),
                        pltpu.VMEM((nq, tk, tq), jnp.int16),
                        pltpu.VMEM((nq, tk, tq), F32),
                        pltpu.VMEM((heads, tk, tq), F32),
                        pltpu.VMEM((heads, tk, tq), F32),
                        pltpu.VMEM((heads, 1, tq), F32),
                        pltpu.VMEM((heads, GROUP_DIM + DENOM_ROWS, tq), F32)],
        compiler_params=_cparams(("parallel", "arbitrary")),
        name="dsa",
    )(qt, k, vt, qit, ki, wt, out_gain)


def _rope_consts(rot_dim, period):
    half = rot_dim // 2
    inv = np.float32(ROPE_THETA) ** (np.float32(-2.0) * np.arange(half, dtype=np.float32) / np.float32(rot_dim))
    lane = np.arange(LANES) % period
    table = np.zeros((8, LANES), np.float32)
    table[0] = np.where(lane < rot_dim, inv[lane % half], 0.0)
    table[1] = np.where((lane >= half) & (lane < rot_dim), 1.0, 0.0)
    table[2] = np.where(lane < half, -1.0, 0.0)
    return jnp.asarray(table)


def kernel(x, c, positions, w_ada, b_ada, norm1_g, ffn1_w1, ffn1_w3, ffn1_w2, norm2_g, w_in, gmlp_v_g, gmlp_ws,
           gmlp_b, q_norm_g, k_norm_g, out_norm_g, w_out, norm3_g, ffn2_w1, ffn2_w3, ffn2_w2):
    bsz, seq, d = x.shape
    depth = w_ada.shape[0]
    aw = gmlp_v_g.shape[1]
    bw = d - aw
    t = bsz * seq

    xf = x.reshape(t, d)
    pos = positions.reshape(t, 1)
    c_pad = jnp.zeros((8, d), F32).at[:bsz].set(c)
    rope_b = _rope_consts(B_ROT_DIM, GROUP_DIM)
    rope_i = _rope_consts(IDX_ROT_DIM, IDX_HEAD_DIM)
    idx_cols = IDX_HEADS * IDX_HEAD_DIM + IDX_HEAD_DIM + IDX_HEADS
    idx_pad = -idx_cols % 256

    for l in range(depth):
        mod = _ada(c_pad, w_ada[l], b_ada[l][None, :])[:bsz].reshape(bsz, 9, d)

        xf = _ffn(xf, mod, norm1_g[l][None, :], ffn1_w1[l].astype(BF16), ffn1_w3[l].astype(BF16),
                  ffn1_w2[l].astype(BF16), row0=0, seq=seq)

        w_all = jnp.concatenate([w_in[l].astype(BF16), jnp.zeros((d, idx_pad), BF16)], axis=1)
        g2 = norm2_g[l][None, :]
        bs = jnp.broadcast_to(gmlp_b[l][:, :, None], gmlp_b.shape[1:] + (GROUP_DIM,))

        ga, qt, k, vt, qit, ki, wt = _proj(
            xf, mod, g2, pos, rope_b, rope_i, w_all, gmlp_v_g[l][None, :], gmlp_ws[l], bs,
            out_norm_g[l][None, :aw], jnp.stack([q_norm_g[l], k_norm_g[l]]), row0=3, seq=seq)
        gb = _dsa(qt, k, vt, qit, ki, wt, out_norm_g[l][None, aw:], bsz=bsz, seq=seq)
        xf = _out(xf, mod, ga, gb, w_out[l].astype(BF16), row0=5, seq=seq)

        xf = _ffn(xf, mod, norm3_g[l][None, :], ffn2_w1[l].astype(BF16), ffn2_w3[l].astype(BF16),
                  ffn2_w2[l].astype(BF16), row0=6, seq=seq)
    return xf.reshape(bsz, seq, d)
```
